```python
import math
import jax, jax.numpy as jnp
from jax import lax
import numpy as np

D_MODEL = 1024
BATCH = 8
SEQ = 4096
DEPTH = 2

GRID_W = 64
ATTN_Q_HEADS = 8
ATTN_KV_HEADS = 2
ATTN_HEAD_DIM = 64
ATTN_WIDTH = ATTN_Q_HEADS * ATTN_HEAD_DIM
ATTN_KV_WIDTH = ATTN_KV_HEADS * ATTN_HEAD_DIM
Q_BLOCK = 128
RET_HEADS = 4
RET_HEAD_DIM = 128
RET_WIDTH = RET_HEADS * RET_HEAD_DIM
RET_CHUNK = 128
N_BRANCHES = 2
ROPE_THETA = 10000.0
EPS = 1e-6
IN_SPLITS = (ATTN_WIDTH, ATTN_KV_WIDTH, ATTN_KV_WIDTH, ATTN_WIDTH,
             RET_WIDTH, RET_WIDTH, RET_WIDTH, RET_WIDTH, N_BRANCHES * D_MODEL)
D_IN = sum(IN_SPLITS)

kernel_name = "hybrid_gqa_retention_gated_encoder"


def rms_norm(x, g):
    xf = x.astype(jnp.float32)
    y = xf * lax.rsqrt(jnp.mean(xf * xf, axis=-1, keepdims=True) + EPS)
    return (y * g.astype(jnp.float32)).astype(x.dtype)


def head_group_norm(o, w):
    mu = jnp.mean(o, axis=-1, keepdims=True)
    var = jnp.mean(jnp.square(o - mu), axis=-1, keepdims=True)
    y = (o - mu) * lax.rsqrt(var + EPS)
    b, s, h, d = o.shape
    return y.reshape(b, s, h * d) * w.astype(jnp.float32)


def axial_rope_tables(seq_len, head_dim):
    n_rows = seq_len // GRID_W
    row = jnp.repeat(jnp.arange(n_rows, dtype=jnp.float32), GRID_W)
    col = jnp.tile(jnp.arange(GRID_W, dtype=jnp.float32), n_rows)
    d_axis = head_dim // 2
    inv_freq = ROPE_THETA ** (-jnp.arange(0, d_axis, 2, dtype=jnp.float32) / d_axis)
    ang = jnp.concatenate([row[:, None] * inv_freq, col[:, None] * inv_freq], axis=-1)
    return jnp.cos(ang), jnp.sin(ang)


def apply_axial_rope(x, cos, sin):
    b, s, h, hd = x.shape
    q = hd // 4
    xf = x.astype(jnp.float32).reshape(b, s, h, 2, 2, q)
    x1 = xf[..., 0, :]
    x2 = xf[..., 1, :]
    c = cos.reshape(s, 1, 2, q)
    sn = sin.reshape(s, 1, 2, q)
    out = jnp.stack([x1 * c - x2 * sn, x1 * sn + x2 * c], axis=-2)
    return out.reshape(b, s, h, hd).astype(x.dtype)


def block_attention(q, k, v):
    b, s, hq, hd = q.shape
    hkv = k.shape[2]
    g = hq // hkv
    nb = s // Q_BLOCK
    qb = q.reshape(b, nb, Q_BLOCK, hkv, g, hd).transpose(1, 0, 2, 3, 4, 5)
    scale = hd ** -0.5

    def one_block(q_blk):
        sc = jnp.einsum('bqkgd,bskd->bkgqs', q_blk, k,
                        preferred_element_type=jnp.float32) * scale
        p = jax.nn.softmax(sc, axis=-1).astype(v.dtype)
        return jnp.einsum('bkgqs,bskd->bqkgd', p, v)

    o = lax.map(one_block, qb)
    return o.transpose(1, 0, 2, 3, 4, 5).reshape(b, s, hq * hd)


def retention_scan(q, k, v, log_gamma, strict):
    b, s, h, dk = q.shape
    dv = v.shape[-1]
    c = RET_CHUNK
    nc = s // c

    def to_chunks(t):
        return t.reshape(b, nc, c, h, t.shape[-1]).transpose(1, 0, 3, 2, 4)

    lg = log_gamma.astype(jnp.float32)[:, None]
    idx = jnp.arange(c, dtype=jnp.float32)
    diff = idx[:, None] - idx[None, :]
    mask = (diff > 0) if strict else (diff >= 0)
    dmat = jnp.where(mask[None], jnp.exp(jnp.maximum(diff, 0.0)[None] * lg[:, :, None]), 0.0)
    q_decay = jnp.exp((idx + 1.0)[None] * lg)
    k_decay = jnp.exp((c - 1.0 - idx)[None] * lg)
    chunk_decay = jnp.exp(c * lg[:, 0])

    def step(state, inp):
        qi, ki, vi = inp
        qf = qi.astype(jnp.float32)
        kf = ki.astype(jnp.float32)
        vf = vi.astype(jnp.float32)
        intra = jnp.einsum('bhij,bhje->bhie', jnp.einsum('bhid,bhjd->bhij', qf, kf) * dmat, vf)
        cross = jnp.einsum('bhid,bhde->bhie', qf, state) * q_decay[..., None]
        state = state * chunk_decay[:, None, None] + jnp.einsum(
            'bhjd,bhje->bhde', kf * k_decay[..., None], vf)
        return state, intra + cross

    state0 = jnp.zeros((b, h, dk, dv), jnp.float32)
    _, o = lax.scan(step, state0, (to_chunks(q), to_chunks(k), to_chunks(v)))
    return o.transpose(1, 0, 3, 2, 4).reshape(b, s, h, dv)


def bidirectional_retention(q, k, v, log_gamma_fwd, log_gamma_bwd):
    fwd = retention_scan(q, k, v, log_gamma_fwd, strict=False)
    flip = lambda t: jnp.flip(t, axis=1)
    bwd = flip(retention_scan(flip(q), flip(k), flip(v), log_gamma_bwd, strict=True))
    return fwd + bwd


def split_columns(z):
    offs = []
    acc = 0
    for w in IN_SPLITS[:-1]:
        acc += w
        offs.append(acc)
    return jnp.split(z, offs, axis=-1)


def setup_inputs(seed: int = 0) -> dict:
    key = jax.random.key(seed)
    ks = jax.random.split(key, 14)
    f32 = jnp.float32
    x = jax.random.normal(ks[0], (BATCH, SEQ, D_MODEL), f32)
    norm_g = 1.0 + 0.02 * jax.random.normal(ks[1], (DEPTH, D_MODEL), f32)
    w_in = jax.random.normal(ks[2], (DEPTH, D_MODEL, D_IN), f32) * D_MODEL ** -0.5
    attn_q_norm = 1.0 + 0.02 * jax.random.normal(ks[3], (DEPTH, ATTN_HEAD_DIM), f32)
    attn_k_norm = 1.0 + 0.02 * jax.random.normal(ks[4], (DEPTH, ATTN_HEAD_DIM), f32)
    base_logit = jnp.log(2.0 ** (5.0 + jnp.arange(RET_HEADS, dtype=f32)) - 1.0)
    ret_decay_fwd = base_logit[None] + 0.1 * jax.random.normal(ks[5], (DEPTH, RET_HEADS), f32)
    ret_decay_bwd = base_logit[None] + 0.1 * jax.random.normal(ks[6], (DEPTH, RET_HEADS), f32)
    ret_gn_w = 1.0 + 0.02 * jax.random.normal(ks[7], (DEPTH, RET_WIDTH), f32)
    w_branch_attn = jax.random.normal(ks[8], (DEPTH, ATTN_WIDTH, D_MODEL), f32) * ATTN_WIDTH ** -0.5
    w_branch_ret = jax.random.normal(ks[9], (DEPTH, RET_WIDTH, D_MODEL), f32) * RET_WIDTH ** -0.5
    w_out = jax.random.normal(ks[10], (DEPTH, D_MODEL, D_MODEL), f32) * D_MODEL ** -0.5
    final_norm_g = 1.0 + 0.02 * jax.random.normal(ks[11], (D_MODEL,), f32)
    return {"x": x, "norm_g": norm_g, "w_in": w_in, "attn_q_norm": attn_q_norm,
            "attn_k_norm": attn_k_norm, "ret_decay_fwd": ret_decay_fwd,
            "ret_decay_bwd": ret_decay_bwd, "ret_gn_w": ret_gn_w,
            "w_branch_attn": w_branch_attn, "w_branch_ret": w_branch_ret,
            "w_out": w_out, "final_norm_g": final_norm_g}


def reference(x, norm_g, w_in, attn_q_norm, attn_k_norm, ret_decay_fwd, ret_decay_bwd,
              ret_gn_w, w_branch_attn, w_branch_ret, w_out, final_norm_g):
    b, s, d = x.shape
    dt = x.dtype
    cos_a, sin_a = axial_rope_tables(s, ATTN_HEAD_DIM)
    cos_r, sin_r = axial_rope_tables(s, RET_HEAD_DIM)

    for layer in range(DEPTH):
        h = rms_norm(x, norm_g[layer])
        z = h @ w_in[layer]
        qa, ka, va, ga, qr, kr, vr, gr, gm = split_columns(z)

        qa = rms_norm(qa.reshape(b, s, ATTN_Q_HEADS, ATTN_HEAD_DIM), attn_q_norm[layer])
        ka = rms_norm(ka.reshape(b, s, ATTN_KV_HEADS, ATTN_HEAD_DIM), attn_k_norm[layer])
        qa = apply_axial_rope(qa, cos_a, sin_a)
        ka = apply_axial_rope(ka, cos_a, sin_a)
        va = va.reshape(b, s, ATTN_KV_HEADS, ATTN_HEAD_DIM)
        oa = block_attention(qa, ka, va)
        ya = (jax.nn.silu(ga) * oa) @ w_branch_attn[layer]

        qr = apply_axial_rope(qr.reshape(b, s, RET_HEADS, RET_HEAD_DIM), cos_r, sin_r)
        kr = apply_axial_rope(kr.reshape(b, s, RET_HEADS, RET_HEAD_DIM), cos_r, sin_r)
        kr = kr * (RET_HEAD_DIM ** -0.5)
        vr = vr.reshape(b, s, RET_HEADS, RET_HEAD_DIM)
        lg_f = jax.nn.log_sigmoid(ret_decay_fwd[layer].astype(jnp.float32))
        lg_b = jax.nn.log_sigmoid(ret_decay_bwd[layer].astype(jnp.float32))
        orr = bidirectional_retention(qr, kr, vr, lg_f, lg_b)
        orr = head_group_norm(orr, ret_gn_w[layer]).astype(dt)
        yb = (jax.nn.silu(gr) * orr) @ w_branch_ret[layer]

        gates = jax.nn.sigmoid(gm.astype(jnp.float32)).astype(dt).reshape(b, s, N_BRANCHES, d)
        merged = gates[:, :, 0] * ya + gates[:, :, 1] * yb
        x = x + merged @ w_out[layer]

    return rms_norm(x, final_norm_g)
```

```python
import functools

import jax
import jax.numpy as jnp
from jax import lax
from jax.experimental import pallas as pl
from jax.experimental.pallas import tpu as pltpu

F32 = jnp.float32
BF16 = jnp.bfloat16

GRID_W = 64
ATTN_Q_HEADS = 8
ATTN_KV_HEADS = 2
ATTN_HEAD_DIM = 64
ATTN_GROUP = ATTN_Q_HEADS // ATTN_KV_HEADS
ATTN_WIDTH = ATTN_Q_HEADS * ATTN_HEAD_DIM
ATTN_KV_WIDTH = ATTN_KV_HEADS * ATTN_HEAD_DIM
GROUP_WIDTH = ATTN_GROUP * ATTN_HEAD_DIM
RET_HEADS = 4
RET_HEAD_DIM = 128
RET_WIDTH = RET_HEADS * RET_HEAD_DIM
RET_CHUNK = 128
ROPE_THETA = 10000.0
EPS = 1e-6
LANES = 128

VMEM_LIMIT_BYTES = 56 * 1024 * 1024

OFF_QA = 0
OFF_KA = OFF_QA + ATTN_WIDTH
OFF_VA = OFF_KA + ATTN_KV_WIDTH
OFF_GA = OFF_VA + ATTN_KV_WIDTH
OFF_QR = OFF_GA + ATTN_WIDTH
OFF_KR = OFF_QR + RET_WIDTH
OFF_VR = OFF_KR + RET_WIDTH
OFF_GR = OFF_VR + RET_WIDTH
OFF_GM = OFF_GR + RET_WIDTH


def _sigmoid(x):
    return 1.0 / (1.0 + jnp.exp(-x))


def _rope_tables(seq_len, head_dim):
    n_rows = seq_len // GRID_W
    row = jnp.repeat(jnp.arange(n_rows, dtype=F32), GRID_W)
    col = jnp.tile(jnp.arange(GRID_W, dtype=F32), n_rows)
    d_axis = head_dim // 2
    inv_freq = ROPE_THETA ** (-jnp.arange(0, d_axis, 2, dtype=F32) / d_axis)
    ang = jnp.concatenate([row[:, None] * inv_freq, col[:, None] * inv_freq], axis=-1)
    quarter = head_dim // 4
    d = jnp.arange(LANES) % head_dim
    axis = d // d_axis
    half = (d % d_axis) // quarter
    freq = d % quarter
    ang_l = ang[:, axis * quarter + freq]
    sign = jnp.where(half == 0, -1.0, 1.0).astype(F32)
    return jnp.cos(ang_l), jnp.sin(ang_l) * sign


def _rope_chunk(xc, cos, sin_signed, quarter):
    lane = lax.broadcasted_iota(jnp.int32, xc.shape, 1)
    first_half = (lane % (2 * quarter)) < quarter
    partner = jnp.where(first_half,
                        pltpu.roll(xc, LANES - quarter, 1),
                        pltpu.roll(xc, quarter, 1))
    return xc * cos + partner * sin_signed


def _in_proj_body(x_ref, g_ref, w_ref, qg_ref, kg_ref, ca_ref, sa_ref, cr_ref, sr_ref, gsum_ref,
                  qa_ref, ka_ref, va_ref, ga_ref, qr_ref, kr_ref, vr_ref, gr_ref, gm_ref):
    x = x_ref[...]
    h = x * lax.rsqrt(jnp.mean(x * x, axis=-1, keepdims=True) + EPS) * g_ref[...]
    hb = h.astype(BF16)

    def proj(off, width):
        return jnp.dot(hb, w_ref[:, off:off + width], preferred_element_type=F32)

    ca, sa = ca_ref[...], sa_ref[...]
    cr, sr = cr_ref[...], sr_ref[...]
    gsum = gsum_ref[...]

    def head_norm_rope(zc, gain):
        sq = zc * zc
        hi = sq.astype(BF16)
        lo = (sq - hi.astype(F32)).astype(BF16)
        ss = jnp.dot(jnp.concatenate([hi, lo], axis=1), gsum, preferred_element_type=F32)
        y = zc * lax.rsqrt(ss * (1.0 / ATTN_HEAD_DIM) + EPS) * gain
        return _rope_chunk(y, ca, sa, ATTN_HEAD_DIM // 4)

    zq = proj(OFF_QA, ATTN_WIDTH)
    qg = qg_ref[...]
    for c in range(ATTN_WIDTH // LANES):
        y = head_norm_rope(zq[:, c * LANES:(c + 1) * LANES], qg)
        qa_ref[:, c * LANES:(c + 1) * LANES] = (y * (ATTN_HEAD_DIM ** -0.5)).astype(BF16)

    zkv = proj(OFF_KA, 2 * ATTN_KV_WIDTH)
    kc = head_norm_rope(zkv[:, :LANES], kg_ref[...])
    vc = zkv[:, LANES:]
    lane = lax.broadcasted_iota(jnp.int32, kc.shape, 1)
    low = lane < ATTN_HEAD_DIM
    for src, dst in ((kc, ka_ref), (vc, va_ref)):
        swapped = pltpu.roll(src, ATTN_HEAD_DIM, 1)
        head0 = jnp.where(low, src, swapped).astype(BF16)
        head1 = jnp.where(low, swapped, src).astype(BF16)
        dst[:, 0 * LANES:1 * LANES] = head0
        dst[:, 1 * LANES:2 * LANES] = head0
        dst[:, 2 * LANES:3 * LANES] = head1
        dst[:, 3 * LANES:4 * LANES] = head1

    zg = proj(OFF_GA, ATTN_WIDTH)
    ga_ref[...] = (zg * _sigmoid(zg)).astype(BF16)

    zq = proj(OFF_QR, RET_WIDTH)
    for c in range(RET_HEADS):
        qr_ref[:, c * LANES:(c + 1) * LANES] = _rope_chunk(
            zq[:, c * LANES:(c + 1) * LANES], cr, sr, RET_HEAD_DIM // 4).astype(BF16)
    zk = proj(OFF_KR, RET_WIDTH)
    for c in range(RET_HEADS):
        y = _rope_chunk(zk[:, c * LANES:(c + 1) * LANES], cr, sr, RET_HEAD_DIM // 4)
        kr_ref[:, c * LANES:(c + 1) * LANES] = (y * (RET_HEAD_DIM ** -0.5)).astype(BF16)
    vr_ref[...] = proj(OFF_VR, RET_WIDTH).astype(BF16)
    zg = proj(OFF_GR, RET_WIDTH)
    gr_ref[...] = (zg * _sigmoid(zg)).astype(BF16)

    d_model = x.shape[1]
    for c in range(2 * d_model // 512):
        zg = proj(OFF_GM + c * 512, 512)
        gm_ref[:, c * 512:(c + 1) * 512] = _sigmoid(zg).astype(BF16)


def _in_proj(x2, g, w, qg, kg, tabs_a, tabs_r, gsum, seq_len, tm):
    n, d = x2.shape
    d_in = w.shape[1]
    blocks_per_seq = seq_len // tm
    row = lambda i: (i, 0)
    const = lambda i: (0, 0)
    tab = lambda i: (i % blocks_per_seq, 0)
    tab_spec = pl.BlockSpec((tm, LANES), tab)
    out_widths = (ATTN_WIDTH, 4 * LANES, 4 * LANES, ATTN_WIDTH,
                  RET_WIDTH, RET_WIDTH, RET_WIDTH, RET_WIDTH, 2 * d)
    return pl.pallas_call(
        _in_proj_body,
        grid=(n // tm,),
        in_specs=[
            pl.BlockSpec((tm, d), row),
            pl.BlockSpec((1, d), const),
            pl.BlockSpec((d, d_in), const, pipeline_mode=pl.Buffered(1)),
            pl.BlockSpec((1, LANES), const),
            pl.BlockSpec((1, LANES), const),
            tab_spec, tab_spec, tab_spec, tab_spec,
            pl.BlockSpec((2 * LANES, LANES), const),
        ],
        out_specs=[pl.BlockSpec((tm, wd), row) for wd in out_widths],
        out_shape=[jax.ShapeDtypeStruct((n, wd), BF16) for wd in out_widths],
        compiler_params=pltpu.CompilerParams(
            dimension_semantics=("arbitrary",), vmem_limit_bytes=VMEM_LIMIT_BYTES),
        name="in_proj",
    )(x2, g, w, qg, kg, *tabs_a, *tabs_r, gsum)


def _attention_body(q_ref, k_ref, v_ref, gate_ref, o_ref, *, tq, tk):
    seq_len = k_ref.shape[0]
    q = q_ref[...]
    lane = lax.broadcasted_iota(jnp.int32, q.shape, 1)
    head_of_lane = lane // ATTN_HEAD_DIM
    qs = jnp.concatenate(
        [jnp.where(head_of_lane == h, q, jnp.zeros_like(q)) for h in range(ATTN_GROUP)], axis=0)

    rows = ATTN_GROUP * tq
    m = jnp.full((rows, 1), -jnp.inf, F32)
    l = jnp.zeros((rows, 1), F32)
    acc = jnp.zeros((rows, GROUP_WIDTH), F32)
    for c in range(seq_len // tk):
        kc = k_ref[c * tk:(c + 1) * tk, :]
        vc = v_ref[c * tk:(c + 1) * tk, :]
        s = lax.dot_general(qs, kc, (((1,), (1,)), ((), ())), preferred_element_type=F32)
        m_new = jnp.maximum(m, jnp.max(s, axis=-1, keepdims=True))
        p = jnp.exp(s - m_new)
        alpha = jnp.exp(m - m_new)
        l = alpha * l + jnp.sum(p, axis=-1, keepdims=True)
        acc = alpha * acc + jnp.dot(p.astype(BF16), vc, preferred_element_type=F32)
        m = m_new
    o = acc / l
    out = jnp.zeros((tq, GROUP_WIDTH), F32)
    for h in range(ATTN_GROUP):
        out = jnp.where(head_of_lane == h, o[h * tq:(h + 1) * tq, :], out)
    o_ref[...] = (out * gate_ref[...].astype(F32)).astype(BF16)


def _attention(qa, ka_rep, va_rep, ga, batch, seq_len, tq, tk):
    n = qa.shape[0]
    qblocks = seq_len // tq
    qmap = lambda b, g, i: (b * qblocks + i, g)
    kvmap = lambda b, g, i: (b, g)
    return pl.pallas_call(
        functools.partial(_attention_body, tq=tq, tk=tk),
        grid=(batch, ATTN_KV_HEADS, qblocks),
        in_specs=[
            pl.BlockSpec((tq, GROUP_WIDTH), qmap),
            pl.BlockSpec((seq_len, GROUP_WIDTH), kvmap),
            pl.BlockSpec((seq_len, GROUP_WIDTH), kvmap),
            pl.BlockSpec((tq, GROUP_WIDTH), qmap),
        ],
        out_specs=pl.BlockSpec((tq, GROUP_WIDTH), qmap),
        out_shape=jax.ShapeDtypeStruct((n, ATTN_WIDTH), BF16),
        compiler_params=pltpu.CompilerParams(
            dimension_semantics=("arbitrary", "arbitrary", "arbitrary"),
            vmem_limit_bytes=VMEM_LIMIT_BYTES),
        name="attention",
    )(qa, ka_rep, va_rep, ga)


def _log_sigmoid(x):
    return jnp.minimum(x, 0.0) - jnp.log1p(jnp.exp(-jnp.abs(x)))


def _retention_body(q_ref, k_ref, v_ref, gate_ref, df_ref, db_ref, gnw_ref, o_ref, sb_ref):
    c = RET_CHUNK
    n_chunks = q_ref.shape[0] // c
    lg_f = _log_sigmoid(df_ref[0])
    lg_b = _log_sigmoid(db_ref[0])
    lg_f = jnp.concatenate([lg_f] * (c // 8), axis=0)
    lg_b = jnp.concatenate([lg_b] * (c // 8), axis=0)
    row = lax.broadcasted_iota(jnp.int32, (c, c), 0).astype(F32)
    col = lax.broadcasted_iota(jnp.int32, (c, c), 1).astype(F32)
    diff = row - col
    dmat = jnp.where(diff >= 0, jnp.exp(jnp.maximum(diff, 0.0) * lg_f),
                     jnp.exp(jnp.maximum(-diff, 0.0) * lg_b))
    qdec_f = jnp.exp((row + 1.0) * lg_f)
    qdec_b = jnp.exp((c - row) * lg_b)
    kdec_f = jnp.exp((c - 1.0 - row) * lg_f)
    kdec_b = jnp.exp(row * lg_b)
    cdec_f = jnp.exp(c * lg_f)
    cdec_b = jnp.exp(c * lg_b)
    gnw = gnw_ref[...]

    def kv_outer(kd, v):
        return lax.dot_general(kd.astype(BF16), v, (((0,), (0,)), ((), ())),
                               preferred_element_type=F32)

    def bwd_step(j, state):
        i = n_chunks - 1 - j
        sl = pl.ds(pl.multiple_of(i * c, c), c)
        sb_ref[i] = state.astype(BF16)
        k = k_ref[sl, :].astype(F32)
        return state * cdec_b + kv_outer(k * kdec_b, v_ref[sl, :])

    lax.fori_loop(0, n_chunks, bwd_step, jnp.zeros((c, c), F32))

    def fwd_step(i, state):
        sl = pl.ds(pl.multiple_of(i * c, c), c)
        q = q_ref[sl, :]
        k = k_ref[sl, :]
        v = v_ref[sl, :]
        a = lax.dot_general(q, k, (((1,), (1,)), ((), ())), preferred_element_type=F32) * dmat
        o = jnp.dot(a.astype(BF16), v, preferred_element_type=F32)
        o = o + jnp.dot(q, state.astype(BF16), preferred_element_type=F32) * qdec_f
        o = o + jnp.dot(q, sb_ref[i], preferred_element_type=F32) * qdec_b
        mu = jnp.mean(o, axis=-1, keepdims=True)
        d = o - mu
        var = jnp.mean(d * d, axis=-1, keepdims=True)
        y = d * lax.rsqrt(var + EPS) * gnw
        o_ref[sl, :] = (y * gate_ref[sl, :].astype(F32)).astype(BF16)
        return state * cdec_f + kv_outer(k.astype(F32) * kdec_f, v)

    lax.fori_loop(0, n_chunks, fwd_step, jnp.zeros((c, c), F32))


def _retention(qr, kr, vr, gr, dec_f, dec_b, gnw, batch, seq_len):
    n = qr.shape[0]
    seq_map = lambda b, h: (b, h)
    head_map = lambda b, h: (h, 0, 0)
    seq_spec = pl.BlockSpec((seq_len, RET_HEAD_DIM), seq_map)
    dec_spec = pl.BlockSpec((1, 8, LANES), head_map)
    return pl.pallas_call(
        _retention_body,
        grid=(batch, RET_HEADS),
        in_specs=[seq_spec, seq_spec, seq_spec, seq_spec, dec_spec, dec_spec,
                  pl.BlockSpec((1, RET_HEAD_DIM), lambda b, h: (0, h))],
        out_specs=seq_spec,
        out_shape=jax.ShapeDtypeStruct((n, RET_WIDTH), BF16),
        scratch_shapes=[pltpu.VMEM((seq_len // RET_CHUNK, RET_HEAD_DIM, RET_HEAD_DIM), BF16)],
        compiler_params=pltpu.CompilerParams(
            dimension_semantics=("arbitrary", "arbitrary"),
            vmem_limit_bytes=VMEM_LIMIT_BYTES),
        name="retention",
    )(qr, kr, vr, gr, dec_f, dec_b, gnw)


def _merge_out_body(ua_ref, ub_ref, gm_ref, x_ref, wa_ref, wb_ref, wo_ref, fg_ref, o_ref, *, final):
    d = x_ref.shape[1]
    ya = jnp.dot(ua_ref[...], wa_ref[...], preferred_element_type=F32)
    yb = jnp.dot(ub_ref[...], wb_ref[...], preferred_element_type=F32)
    merged = gm_ref[:, :d].astype(F32) * ya + gm_ref[:, d:].astype(F32) * yb
    xn = x_ref[...] + jnp.dot(merged.astype(BF16), wo_ref[...], preferred_element_type=F32)
    if final:
        xn = xn * lax.rsqrt(jnp.mean(xn * xn, axis=-1, keepdims=True) + EPS) * fg_ref[...]
    o_ref[...] = xn


def _merge_out(ua, ub, gm, x2, wa, wb, wo, fg, tm, final):
    n, d = x2.shape
    row = lambda i: (i, 0)
    const = lambda i: (0, 0)
    return pl.pallas_call(
        functools.partial(_merge_out_body, final=final),
        grid=(n // tm,),
        in_specs=[
            pl.BlockSpec((tm, ATTN_WIDTH), row),
            pl.BlockSpec((tm, RET_WIDTH), row),
            pl.BlockSpec((tm, 2 * d), row),
            pl.BlockSpec((tm, d), row),
            pl.BlockSpec((ATTN_WIDTH, d), const),
            pl.BlockSpec((RET_WIDTH, d), const),
            pl.BlockSpec((d, d), const),
            pl.BlockSpec((1, d), const),
        ],
        out_specs=pl.BlockSpec((tm, d), row),
        out_shape=jax.ShapeDtypeStruct((n, d), F32),
        compiler_params=pltpu.CompilerParams(
            dimension_semantics=("arbitrary",), vmem_limit_bytes=VMEM_LIMIT_BYTES),
        name="merge_out",
    )(ua, ub, gm, x2, wa, wb, wo, fg)


def _tile_choices(seq_len):
    tm = 512 if seq_len % 512 == 0 else 128
    tq = 128
    tk = 512 if seq_len % 512 == 0 else 128
    return tm, tq, tk


def kernel(x, norm_g, w_in, attn_q_norm, attn_k_norm, ret_decay_fwd, ret_decay_bwd, ret_gn_w,
           w_branch_attn, w_branch_ret, w_out, final_norm_g):
    b, s, d = x.shape
    depth = norm_g.shape[0]
    assert s % RET_CHUNK == 0 and s % GRID_W == 0
    tm, tq, tk = _tile_choices(s)

    tabs_a = _rope_tables(s, ATTN_HEAD_DIM)
    tabs_r = _rope_tables(s, RET_HEAD_DIM)
    head_id = jnp.arange(LANES) // ATTN_HEAD_DIM
    gsum = (head_id[:, None] == head_id[None, :]).astype(BF16)
    gsum = jnp.concatenate([gsum, gsum], axis=0)

    x2 = x.reshape(b * s, d)
    for layer in range(depth):
        qg = jnp.tile(attn_q_norm[layer], LANES // ATTN_HEAD_DIM)[None, :]
        kg = jnp.tile(attn_k_norm[layer], LANES // ATTN_HEAD_DIM)[None, :]
        qa, ka, va, ga, qr, kr, vr, gr, gm = _in_proj(
            x2, norm_g[layer][None, :], w_in[layer].astype(BF16), qg, kg,
            tabs_a, tabs_r, gsum, s, tm)
        ua = _attention(qa, ka, va, ga, b, s, tq, tk)
        dec_f = jnp.broadcast_to(ret_decay_fwd[layer].astype(F32)[:, None, None], (RET_HEADS, 8, LANES))
        dec_b = jnp.broadcast_to(ret_decay_bwd[layer].astype(F32)[:, None, None], (RET_HEADS, 8, LANES))
        ub = _retention(qr, kr, vr, gr, dec_f, dec_b, ret_gn_w[layer][None, :], b, s)
        x2 = _merge_out(ua, ub, gm, x2, w_branch_attn[layer].astype(BF16),
                        w_branch_ret[layer].astype(BF16), w_out[layer].astype(BF16),
                        final_norm_g[None, :], tm, final=(layer == depth - 1))
    return x2.reshape(b, s, d)
```

```python
import functools

import jax
import jax.numpy as jnp
from jax import lax
from jax.experimental import pallas as pl
from jax.experimental.pallas import tpu as pltpu

F32 = jnp.float32
BF16 = jnp.bfloat16

GRID_W = 64
ATTN_Q_HEADS = 8
ATTN_KV_HEADS = 2
ATTN_HEAD_DIM = 64
ATTN_GROUP = ATTN_Q_HEADS // ATTN_KV_HEADS
ATTN_WIDTH = ATTN_Q_HEADS * ATTN_HEAD_DIM
ATTN_KV_WIDTH = ATTN_KV_HEADS * ATTN_HEAD_DIM
GROUP_WIDTH = ATTN_GROUP * ATTN_HEAD_DIM
RET_HEADS = 4
RET_HEAD_DIM = 128
RET_WIDTH = RET_HEADS * RET_HEAD_DIM
RET_CHUNK = 128
RET_HEADS_PER_STEP = 2
ROPE_THETA = 10000.0
EPS = 1e-6
LANES = 128

VMEM_LIMIT_BYTES = 56 * 1024 * 1024

OFF_QA = 0
OFF_KA = OFF_QA + ATTN_WIDTH
OFF_VA = OFF_KA + ATTN_KV_WIDTH
OFF_GA = OFF_VA + ATTN_KV_WIDTH
OFF_QR = OFF_GA + ATTN_WIDTH
OFF_KR = OFF_QR + RET_WIDTH
OFF_VR = OFF_KR + RET_WIDTH
OFF_GR = OFF_VR + RET_WIDTH
OFF_GM = OFF_GR + RET_WIDTH


def _sigmoid(x):
    return 1.0 / (1.0 + jnp.exp(-x))


def _rope_tables(seq_len, head_dim):
    n_rows = seq_len // GRID_W
    row = jnp.repeat(jnp.arange(n_rows, dtype=F32), GRID_W)
    col = jnp.tile(jnp.arange(GRID_W, dtype=F32), n_rows)
    d_axis = head_dim // 2
    inv_freq = ROPE_THETA ** (-jnp.arange(0, d_axis, 2, dtype=F32) / d_axis)
    ang = jnp.concatenate([row[:, None] * inv_freq, col[:, None] * inv_freq], axis=-1)
    quarter = head_dim // 4
    d = jnp.arange(LANES) % head_dim
    axis = d // d_axis
    half = (d % d_axis) // quarter
    freq = d % quarter
    ang_l = ang[:, axis * quarter + freq]
    sign = jnp.where(half == 0, -1.0, 1.0).astype(F32)
    return jnp.cos(ang_l), jnp.sin(ang_l) * sign


def _rope_chunk(xc, cos, sin_signed, quarter):
    lane = lax.broadcasted_iota(jnp.int32, xc.shape, 1)
    first_half = (lane % (2 * quarter)) < quarter
    partner = jnp.where(first_half,
                        pltpu.roll(xc, LANES - quarter, 1),
                        pltpu.roll(xc, quarter, 1))
    return xc * cos + partner * sin_signed


def _in_proj_body(x_ref, g_ref, w_ref, qg_ref, kg_ref, ca_ref, sa_ref, cr_ref, sr_ref, gsum_ref,
                  qa_ref, ka_ref, vt_ref, ga_ref, qr_ref, kr_ref, vr_ref, gr_ref, gm_ref):
    x = x_ref[...]
    h = x * lax.rsqrt(jnp.mean(x * x, axis=-1, keepdims=True) + EPS) * g_ref[...]
    hb = h.astype(BF16)

    def proj(off, width):
        return jnp.dot(hb, w_ref[:, off:off + width], preferred_element_type=F32)

    ca, sa = ca_ref[...], sa_ref[...]
    cr, sr = cr_ref[...], sr_ref[...]
    gsum = gsum_ref[...]

    def head_norm_rope(zc, gain):
        sq = zc * zc
        hi = sq.astype(BF16)
        lo = (sq - hi.astype(F32)).astype(BF16)
        ss = jnp.dot(jnp.concatenate([hi, lo], axis=1), gsum, preferred_element_type=F32)
        y = zc * lax.rsqrt(ss * (1.0 / ATTN_HEAD_DIM) + EPS) * gain
        return _rope_chunk(y, ca, sa, ATTN_HEAD_DIM // 4)

    zq = proj(OFF_QA, ATTN_WIDTH)
    qg = qg_ref[...]
    for c in range(ATTN_WIDTH // LANES):
        y = head_norm_rope(zq[:, c * LANES:(c + 1) * LANES], qg)
        qa_ref[:, c * LANES:(c + 1) * LANES] = (y * (ATTN_HEAD_DIM ** -0.5)).astype(BF16)

    zkv = proj(OFF_KA, 2 * ATTN_KV_WIDTH)
    kc = head_norm_rope(zkv[:, :LANES], kg_ref[...])
    lane = lax.broadcasted_iota(jnp.int32, kc.shape, 1)
    low = lane < ATTN_HEAD_DIM
    swapped = pltpu.roll(kc, ATTN_HEAD_DIM, 1)
    head0 = jnp.where(low, kc, swapped).astype(BF16)
    head1 = jnp.where(low, swapped, kc).astype(BF16)
    ka_ref[:, 0 * LANES:1 * LANES] = head0
    ka_ref[:, 1 * LANES:2 * LANES] = head0
    ka_ref[:, 2 * LANES:3 * LANES] = head1
    ka_ref[:, 3 * LANES:4 * LANES] = head1
    vt = zkv[:, LANES:].T.astype(BF16)
    ones = jnp.ones((ATTN_HEAD_DIM, vt.shape[1]), BF16)
    for g in range(ATTN_KV_HEADS):
        vt_ref[0, 2 * g * ATTN_HEAD_DIM:(2 * g + 1) * ATTN_HEAD_DIM, :] = (
            vt[g * ATTN_HEAD_DIM:(g + 1) * ATTN_HEAD_DIM, :])
        vt_ref[0, (2 * g + 1) * ATTN_HEAD_DIM:(2 * g + 2) * ATTN_HEAD_DIM, :] = ones

    zg = proj(OFF_GA, ATTN_WIDTH)
    ga_ref[...] = (zg * _sigmoid(zg)).astype(BF16)

    zq = proj(OFF_QR, RET_WIDTH)
    for c in range(RET_HEADS):
        qr_ref[:, c * LANES:(c + 1) * LANES] = _rope_chunk(
            zq[:, c * LANES:(c + 1) * LANES], cr, sr, RET_HEAD_DIM // 4).astype(BF16)
    zk = proj(OFF_KR, RET_WIDTH)
    for c in range(RET_HEADS):
        y = _rope_chunk(zk[:, c * LANES:(c + 1) * LANES], cr, sr, RET_HEAD_DIM // 4)
        kr_ref[:, c * LANES:(c + 1) * LANES] = (y * (RET_HEAD_DIM ** -0.5)).astype(BF16)
    vr_ref[...] = proj(OFF_VR, RET_WIDTH).astype(BF16)
    zg = proj(OFF_GR, RET_WIDTH)
    gr_ref[...] = (zg * _sigmoid(zg)).astype(BF16)

    d_model = x.shape[1]
    for c in range(2 * d_model // 512):
        zg = proj(OFF_GM + c * 512, 512)
        gm_ref[:, c * 512:(c + 1) * 512] = _sigmoid(zg).astype(BF16)


def _in_proj(x2, g, w, qg, kg, tabs_a, tabs_r, gsum, seq_len, tm):
    n, d = x2.shape
    d_in = w.shape[1]
    blocks_per_seq = seq_len // tm
    row = lambda i: (i, 0)
    const = lambda i: (0, 0)
    tab = lambda i: (i % blocks_per_seq, 0)
    tab_spec = pl.BlockSpec((tm, LANES), tab)
    def rows_out(width):
        return pl.BlockSpec((tm, width), row), jax.ShapeDtypeStruct((n, width), BF16)

    vt_rows = ATTN_KV_HEADS * 2 * ATTN_HEAD_DIM
    vt_out = (pl.BlockSpec((1, vt_rows, tm), lambda i: (i // blocks_per_seq, 0, i % blocks_per_seq)),
              jax.ShapeDtypeStruct((n // seq_len, vt_rows, seq_len), BF16))
    outs = [rows_out(ATTN_WIDTH), rows_out(4 * LANES), vt_out, rows_out(ATTN_WIDTH),
            rows_out(RET_WIDTH), rows_out(RET_WIDTH), rows_out(RET_WIDTH), rows_out(RET_WIDTH),
            rows_out(2 * d)]
    out_specs = [o[0] for o in outs]
    out_shape = [o[1] for o in outs]
    return pl.pallas_call(
        _in_proj_body,
        grid=(n // tm,),
        in_specs=[
            pl.BlockSpec((tm, d), row),
            pl.BlockSpec((1, d), const),
            pl.BlockSpec((d, d_in), const, pipeline_mode=pl.Buffered(1)),
            pl.BlockSpec((1, LANES), const),
            pl.BlockSpec((1, LANES), const),
            tab_spec, tab_spec, tab_spec, tab_spec,
            pl.BlockSpec((2 * LANES, LANES), const),
        ],
        out_specs=out_specs,
        out_shape=out_shape,
        compiler_params=pltpu.CompilerParams(
            dimension_semantics=("arbitrary",), vmem_limit_bytes=VMEM_LIMIT_BYTES),
        name="in_proj",
    )(x2, g, w, qg, kg, *tabs_a, *tabs_r, gsum)


def _attention_body(q_ref, k_ref, vt_ref, gate_ref, o_ref, *, tq, tk):
    seq_len = k_ref.shape[0]
    q = q_ref[...]
    lane = lax.broadcasted_iota(jnp.int32, q.shape, 1)
    head_of_lane = lane // ATTN_HEAD_DIM
    qs = jnp.concatenate(
        [jnp.where(head_of_lane == h, q, jnp.zeros_like(q)) for h in range(ATTN_GROUP)], axis=0)

    cols = ATTN_GROUP * tq
    m = jnp.full((1, cols), -jnp.inf, F32)
    acc = jnp.zeros((2 * ATTN_HEAD_DIM, cols), F32)
    def scores(c):
        kc = k_ref[c * tk:(c + 1) * tk, :]
        return lax.dot_general(kc, qs, (((1,), (1,)), ((), ())), preferred_element_type=F32)

    n_chunks = seq_len // tk
    st_next = scores(0)
    for c in range(n_chunks):
        st = st_next
        if c + 1 < n_chunks:
            st_next = scores(c + 1)
        m_new = jnp.maximum(m, jnp.max(st, axis=0, keepdims=True))
        pt = jnp.exp(st - m_new).astype(BF16)
        alpha = jnp.exp(m - m_new)
        acc = alpha * acc + jnp.dot(vt_ref[0, :, c * tk:(c + 1) * tk], pt, preferred_element_type=F32)
        m = m_new
    ot = acc * (1.0 / acc[ATTN_HEAD_DIM:ATTN_HEAD_DIM + 1, :])
    t = [ot[:, h * tq:(h + 1) * tq].T for h in range(ATTN_GROUP)]
    low = lax.broadcasted_iota(jnp.int32, t[0].shape, 1) < ATTN_HEAD_DIM
    out = jnp.concatenate(
        [jnp.where(low, t[h], pltpu.roll(t[h + 1], ATTN_HEAD_DIM, 1)) for h in range(0, ATTN_GROUP, 2)],
        axis=1)
    o_ref[...] = (out * gate_ref[...].astype(F32)).astype(BF16)


def _attention(qa, ka_rep, va_t, ga, batch, seq_len, tq, tk):
    n = qa.shape[0]
    qblocks = seq_len // tq
    qmap = lambda b, g, i: (b * qblocks + i, g)
    kvmap = lambda b, g, i: (b, g)
    return pl.pallas_call(
        functools.partial(_attention_body, tq=tq, tk=tk),
        grid=(batch, ATTN_KV_HEADS, qblocks),
        in_specs=[
            pl.BlockSpec((tq, GROUP_WIDTH), qmap),
            pl.BlockSpec((seq_len, GROUP_WIDTH), kvmap),
            pl.BlockSpec((1, 2 * ATTN_HEAD_DIM, seq_len), lambda b, g, i: (b, g, 0)),
            pl.BlockSpec((tq, GROUP_WIDTH), qmap),
        ],
        out_specs=pl.BlockSpec((tq, GROUP_WIDTH), qmap),
        out_shape=jax.ShapeDtypeStruct((n, ATTN_WIDTH), BF16),
        compiler_params=pltpu.CompilerParams(
            dimension_semantics=("arbitrary", "arbitrary", "arbitrary"),
            vmem_limit_bytes=VMEM_LIMIT_BYTES),
        name="attention",
    )(qa, ka_rep, va_t, ga)


def _log_sigmoid(x):
    return jnp.minimum(x, 0.0) - jnp.log1p(jnp.exp(-jnp.abs(x)))


def _retention_body(q_ref, k_ref, v_ref, gate_ref, df_ref, db_ref, gnw_ref, o_ref,
                    u_ref, sf_ref, sb_ref, dmat_ref, rdec_ref, *, heads, c):
    d = RET_HEAD_DIM
    n_chunks = q_ref.shape[0] // c
    row = lax.broadcasted_iota(jnp.int32, (c, c), 0).astype(F32)
    col = lax.broadcasted_iota(jnp.int32, (c, c), 1).astype(F32)
    diff = row - col
    rowd = lax.broadcasted_iota(jnp.int32, (c, d), 0).astype(F32)
    QF, QB, KF, KB = range(4)
    chunk_decay = []
    for h in range(heads):
        lg_f = _log_sigmoid(df_ref[h, 0:1, 0:1])
        lg_b = _log_sigmoid(db_ref[h, 0:1, 0:1])
        dmat_ref[h] = jnp.where(diff >= 0, jnp.exp(jnp.maximum(diff, 0.0) * lg_f),
                                jnp.exp(jnp.maximum(-diff, 0.0) * lg_b))
        rdec_ref[h, QF] = jnp.exp((rowd + 1.0) * lg_f)
        rdec_ref[h, QB] = jnp.exp((c - rowd) * lg_b)
        rdec_ref[h, KF] = jnp.exp((c - 1.0 - rowd) * lg_f)
        rdec_ref[h, KB] = jnp.exp(rowd * lg_b)
        chunk_decay.append((jnp.exp(c * lg_f), jnp.exp(c * lg_b)))

    def outer_step(j, carry):
        sl = pl.ds(pl.multiple_of(j * c, c), c)
        for h in range(heads):
            hs = slice(h * d, (h + 1) * d)
            k = k_ref[sl, hs].astype(F32)
            kk = jnp.concatenate([(k * rdec_ref[h, KF]).astype(BF16),
                                  (k * rdec_ref[h, KB]).astype(BF16)], axis=1)
            u_ref[h, j] = lax.dot_general(kk, v_ref[sl, hs], (((0,), (0,)), ((), ())),
                                          preferred_element_type=F32)
        return carry

    lax.fori_loop(0, n_chunks, outer_step, 0, unroll=2)

    def scan_step(j, states):
        i = n_chunks - 1 - j
        new_states = []
        for h in range(heads):
            s_f, s_b = states[2 * h], states[2 * h + 1]
            sf_ref[h, j] = s_f.astype(BF16)
            sb_ref[h, i] = s_b.astype(BF16)
            new_states.append(s_f * chunk_decay[h][0] + u_ref[h, j, 0:d, :])
            new_states.append(s_b * chunk_decay[h][1] + u_ref[h, i, d:2 * d, :])
        return tuple(new_states)

    lax.fori_loop(0, n_chunks, scan_step, (jnp.zeros((d, d), F32),) * (2 * heads))

    def out_step(i, carry):
        sl = pl.ds(pl.multiple_of(i * c, c), c)
        for h in range(heads):
            hs = slice(h * d, (h + 1) * d)
            q = q_ref[sl, hs]
            v = v_ref[sl, hs]
            a = lax.dot_general(q, k_ref[sl, hs], (((1,), (1,)), ((), ())),
                                preferred_element_type=F32) * dmat_ref[h]
            o = jnp.dot(a.astype(BF16), v, preferred_element_type=F32)
            o = o + jnp.dot(q, sf_ref[h, i], preferred_element_type=F32) * rdec_ref[h, QF]
            o = o + jnp.dot(q, sb_ref[h, i], preferred_element_type=F32) * rdec_ref[h, QB]
            mu = jnp.mean(o, axis=-1, keepdims=True)
            dev = o - mu
            var = jnp.mean(dev * dev, axis=-1, keepdims=True)
            y = dev * lax.rsqrt(var + EPS) * gnw_ref[:, hs]
            o_ref[sl, hs] = (y * gate_ref[sl, hs].astype(F32)).astype(BF16)
        return carry

    lax.fori_loop(0, n_chunks, out_step, 0, unroll=2)


def _retention(qr, kr, vr, gr, dec_f, dec_b, gnw, batch, seq_len, heads, chunk):
    n = qr.shape[0]
    d = RET_HEAD_DIM
    width = heads * d
    n_chunks = seq_len // chunk
    seq_spec = pl.BlockSpec((seq_len, width), lambda b, g: (b, g))
    dec_spec = pl.BlockSpec((heads, 8, LANES), lambda b, g: (g, 0, 0))
    state_scratch = pltpu.VMEM((heads, n_chunks, d, d), BF16)
    return pl.pallas_call(
        functools.partial(_retention_body, heads=heads, c=chunk),
        grid=(batch, RET_HEADS // heads),
        in_specs=[seq_spec, seq_spec, seq_spec, seq_spec, dec_spec, dec_spec,
                  pl.BlockSpec((1, width), lambda b, g: (0, g))],
        out_specs=seq_spec,
        out_shape=jax.ShapeDtypeStruct((n, RET_WIDTH), BF16),
        scratch_shapes=[pltpu.VMEM((heads, n_chunks, 2 * d, d), F32),
                        state_scratch, state_scratch,
                        pltpu.VMEM((heads, chunk, chunk), F32),
                        pltpu.VMEM((heads, 4, chunk, d), F32)],
        compiler_params=pltpu.CompilerParams(
            dimension_semantics=("arbitrary", "arbitrary"),
            vmem_limit_bytes=VMEM_LIMIT_BYTES),
        name="retention",
    )(qr, kr, vr, gr, dec_f, dec_b, gnw)


def _merge_out_body(ua_ref, ub_ref, gm_ref, x_ref, wa_ref, wb_ref, wo_ref, fg_ref, o_ref, *, final):
    d = x_ref.shape[1]
    ya = jnp.dot(ua_ref[...], wa_ref[...], preferred_element_type=F32)
    yb = jnp.dot(ub_ref[...], wb_ref[...], preferred_element_type=F32)
    merged = gm_ref[:, :d].astype(F32) * ya + gm_ref[:, d:].astype(F32) * yb
    xn = x_ref[...] + jnp.dot(merged.astype(BF16), wo_ref[...], preferred_element_type=F32)
    if final:
        xn = xn * lax.rsqrt(jnp.mean(xn * xn, axis=-1, keepdims=True) + EPS) * fg_ref[...]
    o_ref[...] = xn


def _merge_out(ua, ub, gm, x2, wa, wb, wo, fg, tm, final):
    n, d = x2.shape
    row = lambda i: (i, 0)
    const = lambda i: (0, 0)
    return pl.pallas_call(
        functools.partial(_merge_out_body, final=final),
        grid=(n // tm,),
        in_specs=[
            pl.BlockSpec((tm, ATTN_WIDTH), row),
            pl.BlockSpec((tm, RET_WIDTH), row),
            pl.BlockSpec((tm, 2 * d), row),
            pl.BlockSpec((tm, d), row),
            pl.BlockSpec((ATTN_WIDTH, d), const),
            pl.BlockSpec((RET_WIDTH, d), const),
            pl.BlockSpec((d, d), const),
            pl.BlockSpec((1, d), const),
        ],
        out_specs=pl.BlockSpec((tm, d), row),
        out_shape=jax.ShapeDtypeStruct((n, d), F32),
        compiler_params=pltpu.CompilerParams(
            dimension_semantics=("arbitrary",), vmem_limit_bytes=VMEM_LIMIT_BYTES),
        name="merge_out",
    )(ua, ub, gm, x2, wa, wb, wo, fg)


def _tile_choices(seq_len):
    tm = 512 if seq_len % 512 == 0 else 128
    tq = 128
    tk = 512 if seq_len % 512 == 0 else 128
    ret_chunk = 256 if seq_len % 256 == 0 else RET_CHUNK
    return tm, tq, tk, ret_chunk


def kernel(x, norm_g, w_in, attn_q_norm, attn_k_norm, ret_decay_fwd, ret_decay_bwd, ret_gn_w,
           w_branch_attn, w_branch_ret, w_out, final_norm_g):
    b, s, d = x.shape
    depth = norm_g.shape[0]
    assert s % RET_CHUNK == 0 and s % GRID_W == 0
    tm, tq, tk, ret_chunk = _tile_choices(s)

    tabs_a = _rope_tables(s, ATTN_HEAD_DIM)
    tabs_r = _rope_tables(s, RET_HEAD_DIM)
    head_id = jnp.arange(LANES) // ATTN_HEAD_DIM
    gsum = (head_id[:, None] == head_id[None, :]).astype(BF16)
    gsum = jnp.concatenate([gsum, gsum], axis=0)

    x2 = x.reshape(b * s, d)
    for layer in range(depth):
        qg = jnp.tile(attn_q_norm[layer], LANES // ATTN_HEAD_DIM)[None, :]
        kg = jnp.tile(attn_k_norm[layer], LANES // ATTN_HEAD_DIM)[None, :]
        qa, ka, va, ga, qr, kr, vr, gr, gm = _in_proj(
            x2, norm_g[layer][None, :], w_in[layer].astype(BF16), qg, kg,
            tabs_a, tabs_r, gsum, s, tm)
        ua = _attention(qa, ka, va, ga, b, s, tq, tk)
        dec_f = jnp.broadcast_to(ret_decay_fwd[layer].astype(F32)[:, None, None], (RET_HEADS, 8, LANES))
        dec_b = jnp.broadcast_to(ret_decay_bwd[layer].astype(F32)[:, None, None], (RET_HEADS, 8, LANES))
        ub = _retention(qr, kr, vr, gr, dec_f, dec_b, ret_gn_w[layer][None, :], b, s,
                        RET_HEADS_PER_STEP, ret_chunk)
        x2 = _merge_out(ua, ub, gm, x2, w_branch_attn[layer].astype(BF16),
                        w_branch_ret[layer].astype(BF16), w_out[layer].astype(BF16),
                        final_norm_g[None, :], tm, final=(layer == depth - 1))
    return x2.reshape(b, s, d)
```

```python
import functools

import jax
import jax.numpy as jnp
from jax import lax
from jax.experimental import pallas as pl
from jax.experimental.pallas import tpu as pltpu

F32 = jnp.float32
BF16 = jnp.bfloat16

GRID_W = 64
ATTN_Q_HEADS = 8
ATTN_KV_HEADS = 2
ATTN_HEAD_DIM = 64
ATTN_GROUP = ATTN_Q_HEADS // ATTN_KV_HEADS
ATTN_WIDTH = ATTN_Q_HEADS * ATTN_HEAD_DIM
ATTN_KV_WIDTH = ATTN_KV_HEADS * ATTN_HEAD_DIM
GROUP_WIDTH = ATTN_GROUP * ATTN_HEAD_DIM
RET_HEADS = 4
RET_HEAD_DIM = 128
RET_WIDTH = RET_HEADS * RET_HEAD_DIM
RET_CHUNK = 128
RET_HEADS_PER_STEP = 2
ROPE_THETA = 10000.0
EPS = 1e-6
LANES = 128

VMEM_LIMIT_BYTES = 56 * 1024 * 1024

OFF_QA = 0
OFF_KA = OFF_QA + ATTN_WIDTH
OFF_VA = OFF_KA + ATTN_KV_WIDTH
OFF_GA = OFF_VA + ATTN_KV_WIDTH
OFF_QR = OFF_GA + ATTN_WIDTH
OFF_KR = OFF_QR + RET_WIDTH
OFF_VR = OFF_KR + RET_WIDTH
OFF_GR = OFF_VR + RET_WIDTH
OFF_GM = OFF_GR + RET_WIDTH


def _sigmoid(x):
    return 1.0 / (1.0 + jnp.exp(-x))


def _rope_tables(seq_len, head_dim):
    n_rows = seq_len // GRID_W
    row = jnp.repeat(jnp.arange(n_rows, dtype=F32), GRID_W)
    col = jnp.tile(jnp.arange(GRID_W, dtype=F32), n_rows)
    d_axis = head_dim // 2
    inv_freq = ROPE_THETA ** (-jnp.arange(0, d_axis, 2, dtype=F32) / d_axis)
    ang = jnp.concatenate([row[:, None] * inv_freq, col[:, None] * inv_freq], axis=-1)
    quarter = head_dim // 4
    d = jnp.arange(LANES) % head_dim
    axis = d // d_axis
    half = (d % d_axis) // quarter
    freq = d % quarter
    ang_l = ang[:, axis * quarter + freq]
    sign = jnp.where(half == 0, -1.0, 1.0).astype(F32)
    return jnp.cos(ang_l), jnp.sin(ang_l) * sign


def _rope_chunk(xc, cos, sin_signed, quarter):
    lane = lax.broadcasted_iota(jnp.int32, xc.shape, 1)
    first_half = (lane % (2 * quarter)) < quarter
    partner = jnp.where(first_half,
                        pltpu.roll(xc, LANES - quarter, 1),
                        pltpu.roll(xc, quarter, 1))
    return xc * cos + partner * sin_signed


def _in_proj_body(x_ref, g_ref, w_ref, qg_ref, kg_ref, ca_ref, sa_ref, cr_ref, sr_ref, gsum_ref,
                  qa_ref, kt_ref, va_ref, ga_ref, qr_ref, kr_ref, vr_ref, gr_ref, gm_ref):
    x = x_ref[...]
    h = x * lax.rsqrt(jnp.mean(x * x, axis=-1, keepdims=True) + EPS) * g_ref[...]
    hb = h.astype(BF16)

    def proj(off, width):
        return jnp.dot(hb, w_ref[:, off:off + width], preferred_element_type=F32)

    ca, sa = ca_ref[...], sa_ref[...]
    cr, sr = cr_ref[...], sr_ref[...]
    gsum = gsum_ref[...]

    def head_norm_rope(zc, gain):
        sq = zc * zc
        hi = sq.astype(BF16)
        lo = (sq - hi.astype(F32)).astype(BF16)
        ss = jnp.dot(jnp.concatenate([hi, lo], axis=1), gsum, preferred_element_type=F32)
        y = zc * lax.rsqrt(ss * (1.0 / ATTN_HEAD_DIM) + EPS) * gain
        return _rope_chunk(y, ca, sa, ATTN_HEAD_DIM // 4)

    zq = proj(OFF_QA, ATTN_WIDTH)
    qg = qg_ref[...]
    for c in range(ATTN_WIDTH // LANES):
        y = head_norm_rope(zq[:, c * LANES:(c + 1) * LANES], qg)
        qa_ref[:, c * LANES:(c + 1) * LANES] = (y * (ATTN_HEAD_DIM ** -0.5)).astype(BF16)

    zkv = proj(OFF_KA, 2 * ATTN_KV_WIDTH)
    kt = head_norm_rope(zkv[:, :LANES], kg_ref[...]).T.astype(BF16)
    for g in range(ATTN_KV_HEADS):
        head_t = kt[g * ATTN_HEAD_DIM:(g + 1) * ATTN_HEAD_DIM, :]
        for slot in range(ATTN_GROUP):
            r0 = g * GROUP_WIDTH + slot * ATTN_HEAD_DIM
            kt_ref[0, r0:r0 + ATTN_HEAD_DIM, :] = head_t
    vc = zkv[:, LANES:]
    low = lax.broadcasted_iota(jnp.int32, vc.shape, 1) < ATTN_HEAD_DIM
    va_ref[:, :LANES] = jnp.where(low, vc, 1.0).astype(BF16)
    va_ref[:, LANES:] = jnp.where(low, pltpu.roll(vc, ATTN_HEAD_DIM, 1), 1.0).astype(BF16)

    zg = proj(OFF_GA, ATTN_WIDTH)
    ga_ref[...] = (zg * _sigmoid(zg)).astype(BF16)

    zq = proj(OFF_QR, RET_WIDTH)
    for c in range(RET_HEADS):
        qr_ref[:, c * LANES:(c + 1) * LANES] = _rope_chunk(
            zq[:, c * LANES:(c + 1) * LANES], cr, sr, RET_HEAD_DIM // 4).astype(BF16)
    zk = proj(OFF_KR, RET_WIDTH)
    for c in range(RET_HEADS):
        y = _rope_chunk(zk[:, c * LANES:(c + 1) * LANES], cr, sr, RET_HEAD_DIM // 4)
        kr_ref[:, c * LANES:(c + 1) * LANES] = (y * (RET_HEAD_DIM ** -0.5)).astype(BF16)
    vr_ref[...] = proj(OFF_VR, RET_WIDTH).astype(BF16)
    zg = proj(OFF_GR, RET_WIDTH)
    gr_ref[...] = (zg * _sigmoid(zg)).astype(BF16)

    d_model = x.shape[1]
    for c in range(2 * d_model // 512):
        zg = proj(OFF_GM + c * 512, 512)
        gm_ref[:, c * 512:(c + 1) * 512] = _sigmoid(zg).astype(BF16)


def _in_proj(x2, g, w, qg, kg, tabs_a, tabs_r, gsum, seq_len, tm):
    n, d = x2.shape
    d_in = w.shape[1]
    blocks_per_seq = seq_len // tm
    row = lambda i: (i, 0)
    const = lambda i: (0, 0)
    tab = lambda i: (i % blocks_per_seq, 0)
    tab_spec = pl.BlockSpec((tm, LANES), tab)
    def rows_out(width):
        return pl.BlockSpec((tm, width), row), jax.ShapeDtypeStruct((n, width), BF16)

    kt_rows = ATTN_KV_HEADS * GROUP_WIDTH
    kt_out = (pl.BlockSpec((1, kt_rows, tm), lambda i: (i // blocks_per_seq, 0, i % blocks_per_seq)),
              jax.ShapeDtypeStruct((n // seq_len, kt_rows, seq_len), BF16))
    outs = [rows_out(ATTN_WIDTH), kt_out, rows_out(ATTN_KV_HEADS * LANES), rows_out(ATTN_WIDTH),
            rows_out(RET_WIDTH), rows_out(RET_WIDTH), rows_out(RET_WIDTH), rows_out(RET_WIDTH),
            rows_out(2 * d)]
    out_specs = [o[0] for o in outs]
    out_shape = [o[1] for o in outs]
    return pl.pallas_call(
        _in_proj_body,
        grid=(n // tm,),
        in_specs=[
            pl.BlockSpec((tm, d), row),
            pl.BlockSpec((1, d), const),
            pl.BlockSpec((d, d_in), const, pipeline_mode=pl.Buffered(1)),
            pl.BlockSpec((1, LANES), const),
            pl.BlockSpec((1, LANES), const),
            tab_spec, tab_spec, tab_spec, tab_spec,
            pl.BlockSpec((2 * LANES, LANES), const),
        ],
        out_specs=out_specs,
        out_shape=out_shape,
        compiler_params=pltpu.CompilerParams(
            dimension_semantics=("arbitrary",), vmem_limit_bytes=VMEM_LIMIT_BYTES),
        name="in_proj",
    )(x2, g, w, qg, kg, *tabs_a, *tabs_r, gsum)


def _attention_body(qc_ref, qn_ref, kt_ref, va_ref, gate_ref, o_ref, s0_ref, s1_ref, m0_ref, m1_ref,
                    *, tq, tk):
    seq_len = kt_ref.shape[2]
    n_chunks = seq_len // tk
    rows = ATTN_GROUP * tq
    step = pl.program_id(2)

    def stack_heads(q):
        lane = lax.broadcasted_iota(jnp.int32, q.shape, 1)
        return jnp.concatenate(
            [jnp.where(lane // ATTN_HEAD_DIM == h, q, jnp.zeros_like(q)) for h in range(ATTN_GROUP)],
            axis=0)

    def score_chunk(qs, s_ref, c, m_part):
        st = jnp.dot(qs, kt_ref[0, :, c * tk:(c + 1) * tk], preferred_element_type=F32)
        s_ref[:, c * tk:(c + 1) * tk] = st
        for t in range(tk // LANES):
            m_part = jnp.maximum(m_part, st[:, t * LANES:(t + 1) * LANES])
        return m_part

    def row_max(m_part):
        return jnp.broadcast_to(jnp.max(m_part, axis=-1, keepdims=True), (rows, LANES))

    def pv_chunk(s_ref, m_rows, c, acc):
        m_tiled = jnp.concatenate([m_rows] * (tk // LANES), axis=1)
        p = jnp.exp(s_ref[:, c * tk:(c + 1) * tk] - m_tiled).astype(BF16)
        return acc + jnp.dot(p, va_ref[c * tk:(c + 1) * tk, :], preferred_element_type=F32)

    def write_block(acc, r0):
        o = acc * (1.0 / pltpu.roll(acc, ATTN_HEAD_DIM, 1))
        low = lax.broadcasted_iota(jnp.int32, (tq, LANES), 1) < ATTN_HEAD_DIM
        out = jnp.concatenate(
            [jnp.where(low, o[h * tq:(h + 1) * tq], pltpu.roll(o[(h + 1) * tq:(h + 2) * tq], ATTN_HEAD_DIM, 1))
             for h in range(0, ATTN_GROUP, 2)], axis=1)
        o_ref[r0:r0 + tq, :] = (out * gate_ref[r0:r0 + tq, :].astype(F32)).astype(BF16)

    neg_inf = jnp.full((rows, LANES), -jnp.inf, F32)

    @pl.when(step == 0)
    def _():
        qs = stack_heads(qc_ref[0:tq, :])
        m_part = neg_inf
        for c in range(n_chunks):
            m_part = score_chunk(qs, s0_ref, c, m_part)
        m0_ref[...] = m_part

    def half(q_next, s_cur, m_cur, s_next, m_next, r0):
        qs = stack_heads(q_next)
        m_rows = row_max(m_cur[...])
        m_part = neg_inf
        acc = jnp.zeros((rows, LANES), F32)
        for c in range(n_chunks + 1):
            if c < n_chunks:
                acc = pv_chunk(s_cur, m_rows, c, acc)
            if c == n_chunks:
                write_block(acc, r0)
            if c >= 1:
                m_part = score_chunk(qs, s_next, c - 1, m_part)
        m_next[...] = m_part

    half(qc_ref[tq:2 * tq, :], s0_ref, m0_ref, s1_ref, m1_ref, 0)
    half(qn_ref[...], s1_ref, m1_ref, s0_ref, m0_ref, tq)


def _attention(qa, ka_t, va_aug, ga, batch, seq_len, tq, tk):
    n = qa.shape[0]
    qblocks = seq_len // tq
    pairs = qblocks // 2
    pair_map = lambda b, g, i: (b * pairs + i, g)
    next_map = lambda b, g, i: (b * qblocks + jnp.minimum(2 * i + 2, qblocks - 1), g)
    rows = ATTN_GROUP * tq
    return pl.pallas_call(
        functools.partial(_attention_body, tq=tq, tk=tk),
        grid=(batch, ATTN_KV_HEADS, pairs),
        in_specs=[
            pl.BlockSpec((2 * tq, GROUP_WIDTH), pair_map),
            pl.BlockSpec((tq, GROUP_WIDTH), next_map),
            pl.BlockSpec((1, GROUP_WIDTH, seq_len), lambda b, g, i: (b, g, 0)),
            pl.BlockSpec((seq_len, LANES), lambda b, g, i: (b, g)),
            pl.BlockSpec((2 * tq, GROUP_WIDTH), pair_map),
        ],
        out_specs=pl.BlockSpec((2 * tq, GROUP_WIDTH), pair_map),
        out_shape=jax.ShapeDtypeStruct((n, ATTN_WIDTH), BF16),
        scratch_shapes=[pltpu.VMEM((rows, seq_len), F32), pltpu.VMEM((rows, seq_len), F32),
                        pltpu.VMEM((rows, LANES), F32), pltpu.VMEM((rows, LANES), F32)],
        compiler_params=pltpu.CompilerParams(
            dimension_semantics=("arbitrary", "arbitrary", "arbitrary"),
            vmem_limit_bytes=VMEM_LIMIT_BYTES),
        name="attention",
    )(qa, qa, ka_t, va_aug, ga)


def _log_sigmoid(x):
    return jnp.minimum(x, 0.0) - jnp.log1p(jnp.exp(-jnp.abs(x)))


def _retention_body(q_ref, k_ref, v_ref, gate_ref, df_ref, db_ref, gnw_ref, o_ref,
                    u_ref, sf_ref, sb_ref, dmat_ref, rdec_ref, *, heads, c):
    d = RET_HEAD_DIM
    n_chunks = q_ref.shape[0] // c
    row = lax.broadcasted_iota(jnp.int32, (c, c), 0).astype(F32)
    col = lax.broadcasted_iota(jnp.int32, (c, c), 1).astype(F32)
    diff = row - col
    rowd = lax.broadcasted_iota(jnp.int32, (c, d), 0).astype(F32)
    QF, QB, KF, KB = range(4)
    chunk_decay = []
    for h in range(heads):
        lg_f = _log_sigmoid(df_ref[h, 0:1, 0:1])
        lg_b = _log_sigmoid(db_ref[h, 0:1, 0:1])
        dmat_ref[h] = jnp.where(diff >= 0, jnp.exp(jnp.maximum(diff, 0.0) * lg_f),
                                jnp.exp(jnp.maximum(-diff, 0.0) * lg_b))
        rdec_ref[h, QF] = jnp.exp((rowd + 1.0) * lg_f)
        rdec_ref[h, QB] = jnp.exp((c - rowd) * lg_b)
        rdec_ref[h, KF] = jnp.exp((c - 1.0 - rowd) * lg_f)
        rdec_ref[h, KB] = jnp.exp(rowd * lg_b)
        chunk_decay.append((jnp.exp(c * lg_f), jnp.exp(c * lg_b)))

    def outer_step(j, carry):
        sl = pl.ds(pl.multiple_of(j * c, c), c)
        for h in range(heads):
            hs = slice(h * d, (h + 1) * d)
            k = k_ref[sl, hs].astype(F32)
            kk = jnp.concatenate([(k * rdec_ref[h, KF]).astype(BF16),
                                  (k * rdec_ref[h, KB]).astype(BF16)], axis=1)
            u_ref[h, j] = lax.dot_general(kk, v_ref[sl, hs], (((0,), (0,)), ((), ())),
                                          preferred_element_type=F32)
        return carry

    lax.fori_loop(0, n_chunks, outer_step, 0, unroll=2)

    def scan_step(j, states):
        i = n_chunks - 1 - j
        new_states = []
        for h in range(heads):
            s_f, s_b = states[2 * h], states[2 * h + 1]
            sf_ref[h, j] = s_f.astype(BF16)
            sb_ref[h, i] = s_b.astype(BF16)
            new_states.append(s_f * chunk_decay[h][0] + u_ref[h, j, 0:d, :])
            new_states.append(s_b * chunk_decay[h][1] + u_ref[h, i, d:2 * d, :])
        return tuple(new_states)

    lax.fori_loop(0, n_chunks, scan_step, (jnp.zeros((d, d), F32),) * (2 * heads))

    def out_step(i, carry):
        sl = pl.ds(pl.multiple_of(i * c, c), c)
        for h in range(heads):
            hs = slice(h * d, (h + 1) * d)
            q = q_ref[sl, hs]
            v = v_ref[sl, hs]
            a = lax.dot_general(q, k_ref[sl, hs], (((1,), (1,)), ((), ())),
                                preferred_element_type=F32) * dmat_ref[h]
            o = jnp.dot(a.astype(BF16), v, preferred_element_type=F32)
            o = o + jnp.dot(q, sf_ref[h, i], preferred_element_type=F32) * rdec_ref[h, QF]
            o = o + jnp.dot(q, sb_ref[h, i], preferred_element_type=F32) * rdec_ref[h, QB]
            mu = jnp.mean(o, axis=-1, keepdims=True)
            dev = o - mu
            var = jnp.mean(dev * dev, axis=-1, keepdims=True)
            y = dev * lax.rsqrt(var + EPS) * gnw_ref[:, hs]
            o_ref[sl, hs] = (y * gate_ref[sl, hs].astype(F32)).astype(BF16)
        return carry

    lax.fori_loop(0, n_chunks, out_step, 0, unroll=2)


def _retention(qr, kr, vr, gr, dec_f, dec_b, gnw, batch, seq_len, heads, chunk):
    n = qr.shape[0]
    d = RET_HEAD_DIM
    width = heads * d
    n_chunks = seq_len // chunk
    seq_spec = pl.BlockSpec((seq_len, width), lambda b, g: (b, g))
    dec_spec = pl.BlockSpec((heads, 8, LANES), lambda b, g: (g, 0, 0))
    state_scratch = pltpu.VMEM((heads, n_chunks, d, d), BF16)
    return pl.pallas_call(
        functools.partial(_retention_body, heads=heads, c=chunk),
        grid=(batch, RET_HEADS // heads),
        in_specs=[seq_spec, seq_spec, seq_spec, seq_spec, dec_spec, dec_spec,
                  pl.BlockSpec((1, width), lambda b, g: (0, g))],
        out_specs=seq_spec,
        out_shape=jax.ShapeDtypeStruct((n, RET_WIDTH), BF16),
        scratch_shapes=[pltpu.VMEM((heads, n_chunks, 2 * d, d), F32),
                        state_scratch, state_scratch,
                        pltpu.VMEM((heads, chunk, chunk), F32),
                        pltpu.VMEM((heads, 4, chunk, d), F32)],
        compiler_params=pltpu.CompilerParams(
            dimension_semantics=("arbitrary", "arbitrary"),
            vmem_limit_bytes=VMEM_LIMIT_BYTES),
        name="retention",
    )(qr, kr, vr, gr, dec_f, dec_b, gnw)


def _merge_out_body(ua_ref, ub_ref, gm_ref, x_ref, wa_ref, wb_ref, wo_ref, fg_ref, o_ref, *, final):
    d = x_ref.shape[1]
    ya = jnp.dot(ua_ref[...], wa_ref[...], preferred_element_type=F32)
    yb = jnp.dot(ub_ref[...], wb_ref[...], preferred_element_type=F32)
    merged = gm_ref[:, :d].astype(F32) * ya + gm_ref[:, d:].astype(F32) * yb
    xn = x_ref[...] + jnp.dot(merged.astype(BF16), wo_ref[...], preferred_element_type=F32)
    if final:
        xn = xn * lax.rsqrt(jnp.mean(xn * xn, axis=-1, keepdims=True) + EPS) * fg_ref[...]
    o_ref[...] = xn


def _merge_out(ua, ub, gm, x2, wa, wb, wo, fg, tm, final):
    n, d = x2.shape
    row = lambda i: (i, 0)
    const = lambda i: (0, 0)
    return pl.pallas_call(
        functools.partial(_merge_out_body, final=final),
        grid=(n // tm,),
        in_specs=[
            pl.BlockSpec((tm, ATTN_WIDTH), row),
            pl.BlockSpec((tm, RET_WIDTH), row),
            pl.BlockSpec((tm, 2 * d), row),
            pl.BlockSpec((tm, d), row),
            pl.BlockSpec((ATTN_WIDTH, d), const),
            pl.BlockSpec((RET_WIDTH, d), const),
            pl.BlockSpec((d, d), const),
            pl.BlockSpec((1, d), const),
        ],
        out_specs=pl.BlockSpec((tm, d), row),
        out_shape=jax.ShapeDtypeStruct((n, d), F32),
        compiler_params=pltpu.CompilerParams(
            dimension_semantics=("arbitrary",), vmem_limit_bytes=VMEM_LIMIT_BYTES),
        name="merge_out",
    )(ua, ub, gm, x2, wa, wb, wo, fg)


def _tile_choices(seq_len):
    tm = 512 if seq_len % 512 == 0 else 128
    tq = 128
    tk = 512 if seq_len % 512 == 0 else 128
    ret_chunk = 256 if seq_len % 256 == 0 else RET_CHUNK
    return tm, tq, tk, ret_chunk


def kernel(x, norm_g, w_in, attn_q_norm, attn_k_norm, ret_decay_fwd, ret_decay_bwd, ret_gn_w,
           w_branch_attn, w_branch_ret, w_out, final_norm_g):
    b, s, d = x.shape
    depth = norm_g.shape[0]
    assert s % RET_CHUNK == 0 and s % GRID_W == 0
    tm, tq, tk, ret_chunk = _tile_choices(s)
    assert s % (2 * tq) == 0

    tabs_a = _rope_tables(s, ATTN_HEAD_DIM)
    tabs_r = _rope_tables(s, RET_HEAD_DIM)
    head_id = jnp.arange(LANES) // ATTN_HEAD_DIM
    gsum = (head_id[:, None] == head_id[None, :]).astype(BF16)
    gsum = jnp.concatenate([gsum, gsum], axis=0)

    x2 = x.reshape(b * s, d)
    for layer in range(depth):
        qg = jnp.tile(attn_q_norm[layer], LANES // ATTN_HEAD_DIM)[None, :]
        kg = jnp.tile(attn_k_norm[layer], LANES // ATTN_HEAD_DIM)[None, :]
        qa, ka, va, ga, qr, kr, vr, gr, gm = _in_proj(
            x2, norm_g[layer][None, :], w_in[layer].astype(BF16), qg, kg,
            tabs_a, tabs_r, gsum, s, tm)
        ua = _attention(qa, ka, va, ga, b, s, tq, tk)
        dec_f = jnp.broadcast_to(ret_decay_fwd[layer].astype(F32)[:, None, None], (RET_HEADS, 8, LANES))
        dec_b = jnp.broadcast_to(ret_decay_bwd[layer].astype(F32)[:, None, None], (RET_HEADS, 8, LANES))
        ub = _retention(qr, kr, vr, gr, dec_f, dec_b, ret_gn_w[layer][None, :], b, s,
                        RET_HEADS_PER_STEP, ret_chunk)
        x2 = _merge_out(ua, ub, gm, x2, w_branch_attn[layer].astype(BF16),
                        w_branch_ret[layer].astype(BF16), w_out[layer].astype(BF16),
                        final_norm_g[None, :], tm, final=(layer == depth - 1))
    return x2.reshape(b, s, d)
```

```python
import functools

import jax
import jax.numpy as jnp
from jax import lax
from jax.experimental import pallas as pl
from jax.experimental.pallas import tpu as pltpu

F32 = jnp.float32
BF16 = jnp.bfloat16

GRID_W = 64
ATTN_Q_HEADS = 8
ATTN_KV_HEADS = 2
ATTN_HEAD_DIM = 64
ATTN_GROUP = ATTN_Q_HEADS // ATTN_KV_HEADS
ATTN_WIDTH = ATTN_Q_HEADS * ATTN_HEAD_DIM
ATTN_KV_WIDTH = ATTN_KV_HEADS * ATTN_HEAD_DIM
GROUP_WIDTH = ATTN_GROUP * ATTN_HEAD_DIM
RET_HEADS = 4
RET_HEAD_DIM = 128
RET_WIDTH = RET_HEADS * RET_HEAD_DIM
RET_CHUNK = 128
RET_HEADS_PER_STEP = 2
ROPE_THETA = 10000.0
EPS = 1e-6
LANES = 128

VMEM_LIMIT_BYTES = 56 * 1024 * 1024

OFF_QA = 0
OFF_KA = OFF_QA + ATTN_WIDTH
OFF_VA = OFF_KA + ATTN_KV_WIDTH
OFF_GA = OFF_VA + ATTN_KV_WIDTH
OFF_QR = OFF_GA + ATTN_WIDTH
OFF_KR = OFF_QR + RET_WIDTH
OFF_VR = OFF_KR + RET_WIDTH
OFF_GR = OFF_VR + RET_WIDTH
OFF_GM = OFF_GR + RET_WIDTH


def _sigmoid(x):
    return 1.0 / (1.0 + jnp.exp(-x))


def _rope_tables(seq_len, head_dim):
    n_rows = seq_len // GRID_W
    row = jnp.repeat(jnp.arange(n_rows, dtype=F32), GRID_W)
    col = jnp.tile(jnp.arange(GRID_W, dtype=F32), n_rows)
    d_axis = head_dim // 2
    inv_freq = ROPE_THETA ** (-jnp.arange(0, d_axis, 2, dtype=F32) / d_axis)
    ang = jnp.concatenate([row[:, None] * inv_freq, col[:, None] * inv_freq], axis=-1)
    quarter = head_dim // 4
    d = jnp.arange(LANES) % head_dim
    axis = d // d_axis
    half = (d % d_axis) // quarter
    freq = d % quarter
    ang_l = ang[:, axis * quarter + freq]
    sign = jnp.where(half == 0, -1.0, 1.0).astype(F32)
    return jnp.cos(ang_l), jnp.sin(ang_l) * sign


def _rope_chunk(xc, cos, sin_signed, quarter):
    lane = lax.broadcasted_iota(jnp.int32, xc.shape, 1)
    first_half = (lane % (2 * quarter)) < quarter
    partner = jnp.where(first_half,
                        pltpu.roll(xc, LANES - quarter, 1),
                        pltpu.roll(xc, quarter, 1))
    return xc * cos + partner * sin_signed


def _in_proj_body(x_ref, g_ref, w_ref, qg_ref, kg_ref, ca_ref, sa_ref, cr_ref, sr_ref, gsum_ref, gsum2_ref,
                  qa_ref, kt_ref, va_ref, ga_ref, qr_ref, kr_ref, vr_ref, gr_ref, gm_ref):
    x = x_ref[...]
    h = x * lax.rsqrt(jnp.mean(x * x, axis=-1, keepdims=True) + EPS) * g_ref[...]
    hb = h.astype(BF16)

    def proj(off, width):
        return jnp.dot(hb, w_ref[:, off:off + width], preferred_element_type=F32)

    ca, sa = ca_ref[...], sa_ref[...]
    cr, sr = cr_ref[...], sr_ref[...]

    def head_sumsq(z, gsum):
        sq = z * z
        hi = sq.astype(BF16)
        lo = (sq - hi.astype(F32)).astype(BF16)
        return jnp.dot(jnp.concatenate([hi, lo], axis=1), gsum, preferred_element_type=F32)

    def norm_rope(zc, ss, gain):
        y = zc * lax.rsqrt(ss * (1.0 / ATTN_HEAD_DIM) + EPS) * gain
        return _rope_chunk(y, ca, sa, ATTN_HEAD_DIM // 4)

    zq = proj(OFF_QA, ATTN_WIDTH)
    qg = qg_ref[...]
    pair = 2 * LANES
    for c in range(ATTN_WIDTH // pair):
        zp = zq[:, c * pair:(c + 1) * pair]
        ss = head_sumsq(zp, gsum2_ref[...])
        for t in range(2):
            y = norm_rope(zp[:, t * LANES:(t + 1) * LANES], ss[:, t * LANES:(t + 1) * LANES], qg)
            lo_lane = c * pair + t * LANES
            qa_ref[:, lo_lane:lo_lane + LANES] = (y * (ATTN_HEAD_DIM ** -0.5)).astype(BF16)

    zkv = proj(OFF_KA, 2 * ATTN_KV_WIDTH)
    zk = zkv[:, :LANES]
    kt = norm_rope(zk, head_sumsq(zk, gsum_ref[...]), kg_ref[...]).T.astype(BF16)
    for g in range(ATTN_KV_HEADS):
        head_t = kt[g * ATTN_HEAD_DIM:(g + 1) * ATTN_HEAD_DIM, :]
        for slot in range(ATTN_GROUP):
            r0 = g * GROUP_WIDTH + slot * ATTN_HEAD_DIM
            kt_ref[0, r0:r0 + ATTN_HEAD_DIM, :] = head_t
    vc = zkv[:, LANES:]
    low = lax.broadcasted_iota(jnp.int32, vc.shape, 1) < ATTN_HEAD_DIM
    va_ref[:, :LANES] = jnp.where(low, vc, 1.0).astype(BF16)
    va_ref[:, LANES:] = jnp.where(low, pltpu.roll(vc, ATTN_HEAD_DIM, 1), 1.0).astype(BF16)

    zg = proj(OFF_GA, ATTN_WIDTH)
    ga_ref[...] = (zg * _sigmoid(zg)).astype(BF16)

    zq = proj(OFF_QR, RET_WIDTH)
    for c in range(RET_HEADS):
        qr_ref[:, c * LANES:(c + 1) * LANES] = _rope_chunk(
            zq[:, c * LANES:(c + 1) * LANES], cr, sr, RET_HEAD_DIM // 4).astype(BF16)
    zk = proj(OFF_KR, RET_WIDTH)
    for c in range(RET_HEADS):
        y = _rope_chunk(zk[:, c * LANES:(c + 1) * LANES], cr, sr, RET_HEAD_DIM // 4)
        kr_ref[:, c * LANES:(c + 1) * LANES] = (y * (RET_HEAD_DIM ** -0.5)).astype(BF16)
    vr_ref[...] = proj(OFF_VR, RET_WIDTH).astype(BF16)
    zg = proj(OFF_GR, RET_WIDTH)
    gr_ref[...] = (zg * _sigmoid(zg)).astype(BF16)

    d_model = x.shape[1]
    for c in range(2 * d_model // 512):
        zg = proj(OFF_GM + c * 512, 512)
        gm_ref[:, c * 512:(c + 1) * 512] = _sigmoid(zg).astype(BF16)


def _in_proj(x2, g, w, qg, kg, tabs_a, tabs_r, gsum, gsum2, seq_len, tm):
    n, d = x2.shape
    d_in = w.shape[1]
    blocks_per_seq = seq_len // tm
    row = lambda i: (i, 0)
    const = lambda i: (0, 0)
    tab = lambda i: (i % blocks_per_seq, 0)
    tab_spec = pl.BlockSpec((tm, LANES), tab)
    def rows_out(width):
        return pl.BlockSpec((tm, width), row), jax.ShapeDtypeStruct((n, width), BF16)

    kt_rows = ATTN_KV_HEADS * GROUP_WIDTH
    kt_out = (pl.BlockSpec((1, kt_rows, tm), lambda i: (i // blocks_per_seq, 0, i % blocks_per_seq)),
              jax.ShapeDtypeStruct((n // seq_len, kt_rows, seq_len), BF16))
    outs = [rows_out(ATTN_WIDTH), kt_out, rows_out(ATTN_KV_HEADS * LANES), rows_out(ATTN_WIDTH),
            rows_out(RET_WIDTH), rows_out(RET_WIDTH), rows_out(RET_WIDTH), rows_out(RET_WIDTH),
            rows_out(2 * d)]
    out_specs = [o[0] for o in outs]
    out_shape = [o[1] for o in outs]
    return pl.pallas_call(
        _in_proj_body,
        grid=(n // tm,),
        in_specs=[
            pl.BlockSpec((tm, d), row),
            pl.BlockSpec((1, d), const),
            pl.BlockSpec((d, d_in), const, pipeline_mode=pl.Buffered(1)),
            pl.BlockSpec((1, LANES), const),
            pl.BlockSpec((1, LANES), const),
            tab_spec, tab_spec, tab_spec, tab_spec,
            pl.BlockSpec((2 * LANES, LANES), const),
            pl.BlockSpec((4 * LANES, 2 * LANES), const),
        ],
        out_specs=out_specs,
        out_shape=out_shape,
        compiler_params=pltpu.CompilerParams(
            dimension_semantics=("arbitrary",), vmem_limit_bytes=VMEM_LIMIT_BYTES),
        name="in_proj",
    )(x2, g, w, qg, kg, *tabs_a, *tabs_r, gsum, gsum2)


def _attention_body(qc_ref, qn_ref, kt_ref, va_ref, gate_ref, o_ref, s0_ref, s1_ref, m0_ref, m1_ref,
                    *, tq, tk):
    seq_len = kt_ref.shape[2]
    n_chunks = seq_len // tk
    rows = ATTN_GROUP * tq
    step = pl.program_id(2)

    def stack_heads(q):
        lane = lax.broadcasted_iota(jnp.int32, q.shape, 1)
        return jnp.concatenate(
            [jnp.where(lane // ATTN_HEAD_DIM == h, q, jnp.zeros_like(q)) for h in range(ATTN_GROUP)],
            axis=0)

    def score_chunk(qs, s_ref, c, m_part):
        st = jnp.dot(qs, kt_ref[0, :, c * tk:(c + 1) * tk], preferred_element_type=F32)
        s_ref[:, c * tk:(c + 1) * tk] = st
        for t in range(tk // LANES):
            m_part = jnp.maximum(m_part, st[:, t * LANES:(t + 1) * LANES])
        return m_part

    def row_max(m_part):
        return jnp.broadcast_to(jnp.max(m_part, axis=-1, keepdims=True), (rows, LANES))

    def pv_chunk(s_ref, m_rows, c, acc):
        m_tiled = jnp.concatenate([m_rows] * (tk // LANES), axis=1)
        p = jnp.exp(s_ref[:, c * tk:(c + 1) * tk] - m_tiled).astype(BF16)
        return acc + jnp.dot(p, va_ref[c * tk:(c + 1) * tk, :], preferred_element_type=F32)

    def write_block(acc, r0):
        o = acc * (1.0 / pltpu.roll(acc, ATTN_HEAD_DIM, 1))
        low = lax.broadcasted_iota(jnp.int32, (tq, LANES), 1) < ATTN_HEAD_DIM
        out = jnp.concatenate(
            [jnp.where(low, o[h * tq:(h + 1) * tq], pltpu.roll(o[(h + 1) * tq:(h + 2) * tq], ATTN_HEAD_DIM, 1))
             for h in range(0, ATTN_GROUP, 2)], axis=1)
        o_ref[r0:r0 + tq, :] = (out * gate_ref[r0:r0 + tq, :].astype(F32)).astype(BF16)

    neg_inf = jnp.full((rows, LANES), -jnp.inf, F32)

    @pl.when(step == 0)
    def _():
        qs = stack_heads(qc_ref[0:tq, :])
        m_part = neg_inf
        for c in range(n_chunks):
            m_part = score_chunk(qs, s0_ref, c, m_part)
        m0_ref[...] = m_part

    def half(q_next, s_cur, m_cur, s_next, m_next, r0):
        qs = stack_heads(q_next)
        m_rows = row_max(m_cur[...])
        m_part = neg_inf
        acc = jnp.zeros((rows, LANES), F32)
        for c in range(n_chunks + 1):
            if c < n_chunks:
                acc = pv_chunk(s_cur, m_rows, c, acc)
            if c == n_chunks:
                write_block(acc, r0)
            if c >= 1:
                m_part = score_chunk(qs, s_next, c - 1, m_part)
        m_next[...] = m_part

    n_blocks = qc_ref.shape[0] // tq
    for j in range(n_blocks):
        q_next = qc_ref[(j + 1) * tq:(j + 2) * tq, :] if j + 1 < n_blocks else qn_ref[...]
        if j % 2 == 0:
            half(q_next, s0_ref, m0_ref, s1_ref, m1_ref, j * tq)
        else:
            half(q_next, s1_ref, m1_ref, s0_ref, m0_ref, j * tq)


def _attention(qa, ka_t, va_aug, ga, batch, seq_len, tq, tk, nb):
    n = qa.shape[0]
    qblocks = seq_len // tq
    groups = qblocks // nb
    pair_map = lambda b, g, i: (b * groups + i, g)
    next_map = lambda b, g, i: (b * qblocks + jnp.minimum(nb * i + nb, qblocks - 1), g)
    rows = ATTN_GROUP * tq
    return pl.pallas_call(
        functools.partial(_attention_body, tq=tq, tk=tk),
        grid=(batch, ATTN_KV_HEADS, groups),
        in_specs=[
            pl.BlockSpec((nb * tq, GROUP_WIDTH), pair_map),
            pl.BlockSpec((tq, GROUP_WIDTH), next_map),
            pl.BlockSpec((1, GROUP_WIDTH, seq_len), lambda b, g, i: (b, g, 0)),
            pl.BlockSpec((seq_len, LANES), lambda b, g, i: (b, g)),
            pl.BlockSpec((nb * tq, GROUP_WIDTH), pair_map),
        ],
        out_specs=pl.BlockSpec((nb * tq, GROUP_WIDTH), pair_map),
        out_shape=jax.ShapeDtypeStruct((n, ATTN_WIDTH), BF16),
        scratch_shapes=[pltpu.VMEM((rows, seq_len), F32), pltpu.VMEM((rows, seq_len), F32),
                        pltpu.VMEM((rows, LANES), F32), pltpu.VMEM((rows, LANES), F32)],
        compiler_params=pltpu.CompilerParams(
            dimension_semantics=("arbitrary", "arbitrary", "arbitrary"),
            vmem_limit_bytes=VMEM_LIMIT_BYTES),
        name="attention",
    )(qa, qa, ka_t, va_aug, ga)


def _log_sigmoid(x):
    return jnp.minimum(x, 0.0) - jnp.log1p(jnp.exp(-jnp.abs(x)))


def _retention_body(q_ref, k_ref, v_ref, gate_ref, df_ref, db_ref, gnw_ref, o_ref,
                    u_ref, sf_ref, sb_ref, dmat_ref, rdec_ref, *, heads, c):
    d = RET_HEAD_DIM
    n_chunks = q_ref.shape[0] // c
    QF, QB, KF, KB = range(4)
    log_decay = [(_log_sigmoid(df_ref[h, 0:1, 0:1]), _log_sigmoid(db_ref[h, 0:1, 0:1]))
                 for h in range(heads)]
    chunk_decay = [(jnp.exp(c * lg_f), jnp.exp(c * lg_b)) for lg_f, lg_b in log_decay]

    @pl.when(pl.program_id(1) == 0)
    def _():
        row = lax.broadcasted_iota(jnp.int32, (c, c), 0).astype(F32)
        col = lax.broadcasted_iota(jnp.int32, (c, c), 1).astype(F32)
        diff = row - col
        rowd = lax.broadcasted_iota(jnp.int32, (c, d), 0).astype(F32)
        for h, (lg_f, lg_b) in enumerate(log_decay):
            dmat_ref[h] = jnp.where(diff >= 0, jnp.exp(jnp.maximum(diff, 0.0) * lg_f),
                                    jnp.exp(jnp.maximum(-diff, 0.0) * lg_b))
            rdec_ref[h, QF] = jnp.exp((rowd + 1.0) * lg_f)
            rdec_ref[h, QB] = jnp.exp((c - rowd) * lg_b)
            rdec_ref[h, KF] = jnp.exp((c - 1.0 - rowd) * lg_f)
            rdec_ref[h, KB] = jnp.exp(rowd * lg_b)

    def outer_step(j, carry):
        sl = pl.ds(pl.multiple_of(j * c, c), c)
        for h in range(heads):
            hs = slice(h * d, (h + 1) * d)
            k = k_ref[sl, hs].astype(F32)
            kk = jnp.concatenate([(k * rdec_ref[h, KF]).astype(BF16),
                                  (k * rdec_ref[h, KB]).astype(BF16)], axis=1)
            u_ref[h, j] = lax.dot_general(kk, v_ref[sl, hs], (((0,), (0,)), ((), ())),
                                          preferred_element_type=F32)
        return carry

    lax.fori_loop(0, n_chunks, outer_step, 0, unroll=True)

    def scan_step(j, states):
        i = n_chunks - 1 - j
        new_states = []
        for h in range(heads):
            s_f, s_b = states[2 * h], states[2 * h + 1]
            sf_ref[h, j] = s_f.astype(BF16)
            sb_ref[h, i] = s_b.astype(BF16)
            new_states.append(s_f * chunk_decay[h][0] + u_ref[h, j, 0:d, :])
            new_states.append(s_b * chunk_decay[h][1] + u_ref[h, i, d:2 * d, :])
        return tuple(new_states)

    lax.fori_loop(0, n_chunks, scan_step, (jnp.zeros((d, d), F32),) * (2 * heads), unroll=True)

    def out_step(i, carry):
        sl = pl.ds(pl.multiple_of(i * c, c), c)
        for h in range(heads):
            hs = slice(h * d, (h + 1) * d)
            q = q_ref[sl, hs]
            v = v_ref[sl, hs]
            a = lax.dot_general(q, k_ref[sl, hs], (((1,), (1,)), ((), ())),
                                preferred_element_type=F32) * dmat_ref[h]
            o = jnp.dot(a.astype(BF16), v, preferred_element_type=F32)
            o = o + jnp.dot(q, sf_ref[h, i], preferred_element_type=F32) * rdec_ref[h, QF]
            o = o + jnp.dot(q, sb_ref[h, i], preferred_element_type=F32) * rdec_ref[h, QB]
            mu = jnp.mean(o, axis=-1, keepdims=True)
            dev = o - mu
            var = jnp.mean(dev * dev, axis=-1, keepdims=True)
            y = dev * lax.rsqrt(var + EPS) * gnw_ref[:, hs]
            o_ref[sl, hs] = (y * gate_ref[sl, hs].astype(F32)).astype(BF16)
        return carry

    lax.fori_loop(0, n_chunks, out_step, 0, unroll=True)


def _retention(qr, kr, vr, gr, dec_f, dec_b, gnw, batch, seq_len, heads, chunk):
    n = qr.shape[0]
    d = RET_HEAD_DIM
    width = heads * d
    n_chunks = seq_len // chunk
    seq_spec = pl.BlockSpec((seq_len, width), lambda g, b: (b, g))
    dec_spec = pl.BlockSpec((heads, 8, LANES), lambda g, b: (g, 0, 0))
    state_scratch = pltpu.VMEM((heads, n_chunks, d, d), BF16)
    return pl.pallas_call(
        functools.partial(_retention_body, heads=heads, c=chunk),
        grid=(RET_HEADS // heads, batch),
        in_specs=[seq_spec, seq_spec, seq_spec, seq_spec, dec_spec, dec_spec,
                  pl.BlockSpec((1, width), lambda g, b: (0, g))],
        out_specs=seq_spec,
        out_shape=jax.ShapeDtypeStruct((n, RET_WIDTH), BF16),
        scratch_shapes=[pltpu.VMEM((heads, n_chunks, 2 * d, d), F32),
                        state_scratch, state_scratch,
                        pltpu.VMEM((heads, chunk, chunk), F32),
                        pltpu.VMEM((heads, 4, chunk, d), F32)],
        compiler_params=pltpu.CompilerParams(
            dimension_semantics=("arbitrary", "arbitrary"),
            vmem_limit_bytes=VMEM_LIMIT_BYTES),
        name="retention",
    )(qr, kr, vr, gr, dec_f, dec_b, gnw)


def _merge_out_body(ua_ref, ub_ref, gm_ref, x_ref, wa_ref, wb_ref, wo_ref, fg_ref, o_ref, *, final):
    d = x_ref.shape[1]
    ya = jnp.dot(ua_ref[...], wa_ref[...], preferred_element_type=F32)
    yb = jnp.dot(ub_ref[...], wb_ref[...], preferred_element_type=F32)
    merged = gm_ref[:, :d].astype(F32) * ya + gm_ref[:, d:].astype(F32) * yb
    xn = x_ref[...] + jnp.dot(merged.astype(BF16), wo_ref[...], preferred_element_type=F32)
    if final:
        xn = xn * lax.rsqrt(jnp.mean(xn * xn, axis=-1, keepdims=True) + EPS) * fg_ref[...]
    o_ref[...] = xn


def _merge_out(ua, ub, gm, x2, wa, wb, wo, fg, tm, final):
    n, d = x2.shape
    row = lambda i: (i, 0)
    const = lambda i: (0, 0)
    return pl.pallas_call(
        functools.partial(_merge_out_body, final=final),
        grid=(n // tm,),
        in_specs=[
            pl.BlockSpec((tm, ATTN_WIDTH), row),
            pl.BlockSpec((tm, RET_WIDTH), row),
            pl.BlockSpec((tm, 2 * d), row),
            pl.BlockSpec((tm, d), row),
            pl.BlockSpec((ATTN_WIDTH, d), const),
            pl.BlockSpec((RET_WIDTH, d), const),
            pl.BlockSpec((d, d), const),
            pl.BlockSpec((1, d), const),
        ],
        out_specs=pl.BlockSpec((tm, d), row),
        out_shape=jax.ShapeDtypeStruct((n, d), F32),
        compiler_params=pltpu.CompilerParams(
            dimension_semantics=("arbitrary",), vmem_limit_bytes=VMEM_LIMIT_BYTES),
        name="merge_out",
    )(ua, ub, gm, x2, wa, wb, wo, fg)


def _tile_choices(seq_len):
    tm = 512 if seq_len % 512 == 0 else 128
    tm_out = 1024 if seq_len % 1024 == 0 else tm
    tq = 128
    tk = 512 if seq_len % 512 == 0 else 128
    ret_chunk = 256 if seq_len % 256 == 0 else RET_CHUNK
    attn_blocks = 4 if seq_len % (4 * tq) == 0 else 2
    return tm, tm_out, tq, tk, ret_chunk, attn_blocks


def kernel(x, norm_g, w_in, attn_q_norm, attn_k_norm, ret_decay_fwd, ret_decay_bwd, ret_gn_w,
           w_branch_attn, w_branch_ret, w_out, final_norm_g):
    b, s, d = x.shape
    depth = norm_g.shape[0]
    assert s % RET_CHUNK == 0 and s % GRID_W == 0
    tm, tm_out, tq, tk, ret_chunk, attn_blocks = _tile_choices(s)
    assert s % (attn_blocks * tq) == 0

    tabs_a = _rope_tables(s, ATTN_HEAD_DIM)
    tabs_r = _rope_tables(s, RET_HEAD_DIM)
    head_id = jnp.arange(LANES) // ATTN_HEAD_DIM
    gsum = (head_id[:, None] == head_id[None, :]).astype(BF16)
    gsum = jnp.concatenate([gsum, gsum], axis=0)
    head_id2 = jnp.arange(2 * LANES) // ATTN_HEAD_DIM
    gsum2 = (head_id2[:, None] == head_id2[None, :]).astype(BF16)
    gsum2 = jnp.concatenate([gsum2, gsum2], axis=0)

    x2 = x.reshape(b * s, d)
    for layer in range(depth):
        qg = jnp.tile(attn_q_norm[layer], LANES // ATTN_HEAD_DIM)[None, :]
        kg = jnp.tile(attn_k_norm[layer], LANES // ATTN_HEAD_DIM)[None, :]
        qa, ka, va, ga, qr, kr, vr, gr, gm = _in_proj(
            x2, norm_g[layer][None, :], w_in[layer].astype(BF16), qg, kg,
            tabs_a, tabs_r, gsum, gsum2, s, tm)
        ua = _attention(qa, ka, va, ga, b, s, tq, tk, attn_blocks)
        dec_f = jnp.broadcast_to(ret_decay_fwd[layer].astype(F32)[:, None, None], (RET_HEADS, 8, LANES))
        dec_b = jnp.broadcast_to(ret_decay_bwd[layer].astype(F32)[:, None, None], (RET_HEADS, 8, LANES))
        ub = _retention(qr, kr, vr, gr, dec_f, dec_b, ret_gn_w[layer][None, :], b, s,
                        RET_HEADS_PER_STEP, ret_chunk)
        x2 = _merge_out(ua, ub, gm, x2, w_branch_attn[layer].astype(BF16),
                        w_branch_ret[layer].astype(BF16), w_out[layer].astype(BF16),
                        final_norm_g[None, :], tm_out, final=(layer == depth - 1))
    return x2.reshape(b, s, d)
```

```python
import functools

import jax
import jax.numpy as jnp
import numpy as np
from jax import lax
from jax.experimental import pallas as pl
from jax.experimental.pallas import tpu as pltpu

F32 = jnp.float32
BF16 = jnp.bfloat16

GRID_W = 64
ATTN_Q_HEADS = 8
ATTN_KV_HEADS = 2
ATTN_HEAD_DIM = 64
ATTN_GROUP = ATTN_Q_HEADS // ATTN_KV_HEADS
ATTN_WIDTH = ATTN_Q_HEADS * ATTN_HEAD_DIM
ATTN_KV_WIDTH = ATTN_KV_HEADS * ATTN_HEAD_DIM
GROUP_WIDTH = ATTN_GROUP * ATTN_HEAD_DIM
RET_HEADS = 4
RET_HEAD_DIM = 128
RET_WIDTH = RET_HEADS * RET_HEAD_DIM
RET_CHUNK = 128
RET_HEADS_PER_STEP = 2
ROPE_THETA = 10000.0
EPS = 1e-6
LANES = 128

VMEM_LIMIT_BYTES = 56 * 1024 * 1024

OFF_QA = 0
OFF_KA = OFF_QA + ATTN_WIDTH
OFF_VA = OFF_KA + ATTN_KV_WIDTH
OFF_GA = OFF_VA + ATTN_KV_WIDTH
OFF_QR = OFF_GA + ATTN_WIDTH
OFF_KR = OFF_QR + RET_WIDTH
OFF_VR = OFF_KR + RET_WIDTH
OFF_GR = OFF_VR + RET_WIDTH
OFF_GM = OFF_GR + RET_WIDTH


def _sigmoid(x):
    return 1.0 / (1.0 + jnp.exp(-x))


def _rope_tables(seq_len, head_dim):
    n_rows = seq_len // GRID_W
    row = np.repeat(np.arange(n_rows, dtype=np.float64), GRID_W)
    col = np.tile(np.arange(GRID_W, dtype=np.float64), n_rows)
    d_axis = head_dim // 2
    inv_freq = ROPE_THETA ** (-np.arange(0, d_axis, 2, dtype=np.float64) / d_axis)
    ang = np.concatenate([row[:, None] * inv_freq, col[:, None] * inv_freq], axis=-1)
    quarter = head_dim // 4
    d = np.arange(LANES) % head_dim
    axis = d // d_axis
    half = (d % d_axis) // quarter
    freq = d % quarter
    ang_l = ang[:, axis * quarter + freq]
    sign = np.where(half == 0, -1.0, 1.0)
    return (jnp.asarray(np.cos(ang_l), dtype=F32), jnp.asarray(np.sin(ang_l) * sign, dtype=F32))


def _rope_chunk(xc, cos, sin_signed, quarter):
    lane = lax.broadcasted_iota(jnp.int32, xc.shape, 1)
    first_half = (lane % (2 * quarter)) < quarter
    partner = jnp.where(first_half,
                        pltpu.roll(xc, LANES - quarter, 1),
                        pltpu.roll(xc, quarter, 1))
    return xc * cos + partner * sin_signed


def _in_proj_body(x_ref, g_ref, w_ref, qg_ref, kg_ref, ca_ref, sa_ref, cr_ref, sr_ref, gsum_ref, gsum2_ref,
                  qa_ref, kt_ref, va_ref, ga_ref, qr_ref, kr_ref, vr_ref, gr_ref, gm_ref):
    x = x_ref[...]
    h = x * lax.rsqrt(jnp.mean(x * x, axis=-1, keepdims=True) + EPS) * g_ref[...]
    hb = h.astype(BF16)

    def proj(off, width):
        return jnp.dot(hb, w_ref[0, :, off:off + width].astype(BF16), preferred_element_type=F32)

    ca, sa = ca_ref[...], sa_ref[...]
    cr, sr = cr_ref[...], sr_ref[...]

    def head_sumsq(z, gsum):
        sq = z * z
        hi = sq.astype(BF16)
        lo = (sq - hi.astype(F32)).astype(BF16)
        return jnp.dot(jnp.concatenate([hi, lo], axis=1), gsum, preferred_element_type=F32)

    def norm_rope(zc, ss, gain):
        y = zc * lax.rsqrt(ss * (1.0 / ATTN_HEAD_DIM) + EPS) * gain
        return _rope_chunk(y, ca, sa, ATTN_HEAD_DIM // 4)

    zq = proj(OFF_QA, ATTN_WIDTH)
    qg = qg_ref[...]
    pair = 2 * LANES
    for c in range(ATTN_WIDTH // pair):
        zp = zq[:, c * pair:(c + 1) * pair]
        ss = head_sumsq(zp, gsum2_ref[...])
        for t in range(2):
            y = norm_rope(zp[:, t * LANES:(t + 1) * LANES], ss[:, t * LANES:(t + 1) * LANES], qg)
            lo_lane = c * pair + t * LANES
            qa_ref[:, lo_lane:lo_lane + LANES] = (y * (ATTN_HEAD_DIM ** -0.5 * 1.4426950408889634)).astype(BF16)

    zkv = proj(OFF_KA, 2 * ATTN_KV_WIDTH)
    zk = zkv[:, :LANES]
    kt = norm_rope(zk, head_sumsq(zk, gsum_ref[...]), kg_ref[...]).T.astype(BF16)
    for g in range(ATTN_KV_HEADS):
        head_t = kt[g * ATTN_HEAD_DIM:(g + 1) * ATTN_HEAD_DIM, :]
        for slot in range(ATTN_GROUP):
            r0 = g * GROUP_WIDTH + slot * ATTN_HEAD_DIM
            kt_ref[0, r0:r0 + ATTN_HEAD_DIM, :] = head_t
    vc = zkv[:, LANES:]
    low = lax.broadcasted_iota(jnp.int32, vc.shape, 1) < ATTN_HEAD_DIM
    va_ref[:, :LANES] = jnp.where(low, vc, 1.0).astype(BF16)
    va_ref[:, LANES:] = jnp.where(low, pltpu.roll(vc, ATTN_HEAD_DIM, 1), 1.0).astype(BF16)

    zg = proj(OFF_GA, ATTN_WIDTH)
    ga_ref[...] = (zg * _sigmoid(zg)).astype(BF16)

    zq = proj(OFF_QR, RET_WIDTH)
    for c in range(RET_HEADS):
        qr_ref[:, c * LANES:(c + 1) * LANES] = _rope_chunk(
            zq[:, c * LANES:(c + 1) * LANES], cr, sr, RET_HEAD_DIM // 4).astype(BF16)
    zk = proj(OFF_KR, RET_WIDTH)
    for c in range(RET_HEADS):
        y = _rope_chunk(zk[:, c * LANES:(c + 1) * LANES], cr, sr, RET_HEAD_DIM // 4)
        kr_ref[:, c * LANES:(c + 1) * LANES] = (y * (RET_HEAD_DIM ** -0.5)).astype(BF16)
    zg = proj(OFF_GR, RET_WIDTH)
    gr_ref[...] = (zg * _sigmoid(zg)).astype(BF16)

    d_model = x.shape[1]
    for c in range(2 * d_model // 512):
        zg = proj(OFF_GM + c * 512, 512)
        gm_ref[:, c * 512:(c + 1) * 512] = _sigmoid(zg).astype(BF16)
    vr_ref[...] = proj(OFF_VR, RET_WIDTH).astype(BF16)


def _in_proj(x2, g, w, layer, qg, kg, tabs_a, tabs_r, gsum, gsum2, seq_len, tm):
    n, d = x2.shape
    d_in = w.shape[2]
    blocks_per_seq = seq_len // tm
    row = lambda i: (i, 0)
    const = lambda i: (0, 0)
    tab = lambda i: (i % blocks_per_seq, 0)
    tab_spec = pl.BlockSpec((tm, LANES), tab)
    def rows_out(width):
        return pl.BlockSpec((tm, width), row), jax.ShapeDtypeStruct((n, width), BF16)

    kt_rows = ATTN_KV_HEADS * GROUP_WIDTH
    kt_out = (pl.BlockSpec((1, kt_rows, tm), lambda i: (i // blocks_per_seq, 0, i % blocks_per_seq)),
              jax.ShapeDtypeStruct((n // seq_len, kt_rows, seq_len), BF16))
    outs = [rows_out(ATTN_WIDTH), kt_out, rows_out(ATTN_KV_HEADS * LANES), rows_out(ATTN_WIDTH),
            rows_out(RET_WIDTH), rows_out(RET_WIDTH), rows_out(RET_WIDTH), rows_out(RET_WIDTH),
            rows_out(2 * d)]
    out_specs = [o[0] for o in outs]
    out_shape = [o[1] for o in outs]
    return pl.pallas_call(
        _in_proj_body,
        grid=(n // tm,),
        in_specs=[
            pl.BlockSpec((tm, d), row),
            pl.BlockSpec((1, d), const),
            pl.BlockSpec((1, d, d_in), lambda i: (layer, 0, 0), pipeline_mode=pl.Buffered(1)),
            pl.BlockSpec((1, LANES), const),
            pl.BlockSpec((1, LANES), const),
            tab_spec, tab_spec, tab_spec, tab_spec,
            pl.BlockSpec((2 * LANES, LANES), const),
            pl.BlockSpec((4 * LANES, 2 * LANES), const),
        ],
        out_specs=out_specs,
        out_shape=out_shape,
        compiler_params=pltpu.CompilerParams(
            dimension_semantics=("arbitrary",), vmem_limit_bytes=VMEM_LIMIT_BYTES),
        name="in_proj",
    )(x2, g, w, qg, kg, *tabs_a, *tabs_r, gsum, gsum2)


def _attention_body(qc_ref, qn_ref, kt_ref, va_ref, gate_ref, o_ref, s0_ref, s1_ref, m0_ref, m1_ref,
                    *, tq, tk):
    seq_len = kt_ref.shape[2]
    n_chunks = seq_len // tk
    rows = ATTN_GROUP * tq
    step = pl.program_id(2)

    def stack_heads(q):
        lane = lax.broadcasted_iota(jnp.int32, q.shape, 1)
        return jnp.concatenate(
            [jnp.where(lane // ATTN_HEAD_DIM == h, q, jnp.zeros_like(q)) for h in range(ATTN_GROUP)],
            axis=0)

    def score_chunk(qs, s_ref, c, m_part):
        st = jnp.dot(qs, kt_ref[0, :, c * tk:(c + 1) * tk], preferred_element_type=F32)
        s_ref[:, c * tk:(c + 1) * tk] = st
        for t in range(tk // LANES):
            m_part = jnp.maximum(m_part, st[:, t * LANES:(t + 1) * LANES])
        return m_part

    def row_max(m_part):
        return jnp.broadcast_to(jnp.max(m_part, axis=-1, keepdims=True), (rows, LANES))

    def pv_chunk(s_ref, m_rows, c, acc):
        m_tiled = jnp.concatenate([m_rows] * (tk // LANES), axis=1)
        p = jnp.exp2(s_ref[:, c * tk:(c + 1) * tk] - m_tiled).astype(BF16)
        return acc + jnp.dot(p, va_ref[c * tk:(c + 1) * tk, :], preferred_element_type=F32)

    def write_block(acc, r0):
        o = acc * (1.0 / pltpu.roll(acc, ATTN_HEAD_DIM, 1))
        low = lax.broadcasted_iota(jnp.int32, (tq, LANES), 1) < ATTN_HEAD_DIM
        out = jnp.concatenate(
            [jnp.where(low, o[h * tq:(h + 1) * tq], pltpu.roll(o[(h + 1) * tq:(h + 2) * tq], ATTN_HEAD_DIM, 1))
             for h in range(0, ATTN_GROUP, 2)], axis=1)
        o_ref[r0:r0 + tq, :] = (out * gate_ref[r0:r0 + tq, :].astype(F32)).astype(BF16)

    neg_inf = jnp.full((rows, LANES), -jnp.inf, F32)

    @pl.when(step == 0)
    def _():
        qs = stack_heads(qc_ref[0:tq, :])
        m_part = neg_inf
        for c in range(n_chunks):
            m_part = score_chunk(qs, s0_ref, c, m_part)
        m0_ref[...] = m_part

    def half(q_next, s_cur, m_cur, s_next, m_next, r0):
        qs = stack_heads(q_next)
        m_rows = row_max(m_cur[...])
        m_part = neg_inf
        acc = jnp.zeros((rows, LANES), F32)
        for c in range(n_chunks + 1):
            if c < n_chunks:
                acc = pv_chunk(s_cur, m_rows, c, acc)
            if c == n_chunks:
                write_block(acc, r0)
            if c >= 1:
                m_part = score_chunk(qs, s_next, c - 1, m_part)
        m_next[...] = m_part

    n_blocks = qc_ref.shape[0] // tq
    for j in range(n_blocks):
        q_next = qc_ref[(j + 1) * tq:(j + 2) * tq, :] if j + 1 < n_blocks else qn_ref[...]
        if j % 2 == 0:
            half(q_next, s0_ref, m0_ref, s1_ref, m1_ref, j * tq)
        else:
            half(q_next, s1_ref, m1_ref, s0_ref, m0_ref, j * tq)


def _attention(qa, ka_t, va_aug, ga, batch, seq_len, tq, tk, nb):
    n = qa.shape[0]
    qblocks = seq_len // tq
    groups = qblocks // nb
    pair_map = lambda b, g, i: (b * groups + i, g)
    next_map = lambda b, g, i: (b * qblocks + jnp.minimum(nb * i + nb, qblocks - 1), g)
    rows = ATTN_GROUP * tq
    return pl.pallas_call(
        functools.partial(_attention_body, tq=tq, tk=tk),
        grid=(batch, ATTN_KV_HEADS, groups),
        in_specs=[
            pl.BlockSpec((nb * tq, GROUP_WIDTH), pair_map),
            pl.BlockSpec((tq, GROUP_WIDTH), next_map),
            pl.BlockSpec((1, GROUP_WIDTH, seq_len), lambda b, g, i: (b, g, 0)),
            pl.BlockSpec((seq_len, LANES), lambda b, g, i: (b, g)),
            pl.BlockSpec((nb * tq, GROUP_WIDTH), pair_map),
        ],
        out_specs=pl.BlockSpec((nb * tq, GROUP_WIDTH), pair_map),
        out_shape=jax.ShapeDtypeStruct((n, ATTN_WIDTH), BF16),
        scratch_shapes=[pltpu.VMEM((rows, seq_len), F32), pltpu.VMEM((rows, seq_len), F32),
                        pltpu.VMEM((rows, LANES), F32), pltpu.VMEM((rows, LANES), F32)],
        compiler_params=pltpu.CompilerParams(
            dimension_semantics=("arbitrary", "arbitrary", "arbitrary"),
            vmem_limit_bytes=VMEM_LIMIT_BYTES),
        name="attention",
    )(qa, qa, ka_t, va_aug, ga)


def _log_sigmoid(x):
    return jnp.minimum(x, 0.0) - jnp.log1p(jnp.exp(-jnp.abs(x)))


def _retention_body(q_ref, k_ref, v_ref, gate_ref, df_ref, db_ref, gnw_ref, o_ref,
                    u_ref, sf_ref, sb_ref, dmat_ref, rdec_ref, *, heads, c):
    d = RET_HEAD_DIM
    n_chunks = q_ref.shape[0] // c
    QF, QB, KF, KB = range(4)
    log_decay = [(_log_sigmoid(df_ref[h, 0:1, 0:1]), _log_sigmoid(db_ref[h, 0:1, 0:1]))
                 for h in range(heads)]
    chunk_decay = [(jnp.exp(c * lg_f), jnp.exp(c * lg_b)) for lg_f, lg_b in log_decay]

    @pl.when(pl.program_id(1) == 0)
    def _():
        row = lax.broadcasted_iota(jnp.int32, (c, c), 0).astype(F32)
        col = lax.broadcasted_iota(jnp.int32, (c, c), 1).astype(F32)
        diff = row - col
        rowd = lax.broadcasted_iota(jnp.int32, (c, d), 0).astype(F32)
        for h, (lg_f, lg_b) in enumerate(log_decay):
            dmat_ref[h] = jnp.where(diff >= 0, jnp.exp(jnp.maximum(diff, 0.0) * lg_f),
                                    jnp.exp(jnp.maximum(-diff, 0.0) * lg_b))
            rdec_ref[h, QF] = jnp.exp((rowd + 1.0) * lg_f)
            rdec_ref[h, QB] = jnp.exp((c - rowd) * lg_b)
            rdec_ref[h, KF] = jnp.exp((c - 1.0 - rowd) * lg_f)
            rdec_ref[h, KB] = jnp.exp(rowd * lg_b)

    def outer_step(j, carry):
        sl = pl.ds(pl.multiple_of(j * c, c), c)
        for h in range(heads):
            hs = slice(h * d, (h + 1) * d)
            k = k_ref[sl, hs].astype(F32)
            kk = jnp.concatenate([(k * rdec_ref[h, KF]).astype(BF16),
                                  (k * rdec_ref[h, KB]).astype(BF16)], axis=1)
            u_ref[h, j] = lax.dot_general(kk, v_ref[sl, hs], (((0,), (0,)), ((), ())),
                                          preferred_element_type=F32)
        return carry

    lax.fori_loop(0, n_chunks, outer_step, 0, unroll=True)

    def scan_step(j, states):
        i = n_chunks - 1 - j
        new_states = []
        for h in range(heads):
            s_f, s_b = states[2 * h], states[2 * h + 1]
            sf_ref[h, j] = s_f.astype(BF16)
            sb_ref[h, i] = s_b.astype(BF16)
            new_states.append(s_f * chunk_decay[h][0] + u_ref[h, j, 0:d, :])
            new_states.append(s_b * chunk_decay[h][1] + u_ref[h, i, d:2 * d, :])
        return tuple(new_states)

    lax.fori_loop(0, n_chunks, scan_step, (jnp.zeros((d, d), F32),) * (2 * heads), unroll=True)

    def out_step(i, carry):
        sl = pl.ds(pl.multiple_of(i * c, c), c)
        for h in range(heads):
            hs = slice(h * d, (h + 1) * d)
            q = q_ref[sl, hs]
            v = v_ref[sl, hs]
            a = lax.dot_general(q, k_ref[sl, hs], (((1,), (1,)), ((), ())),
                                preferred_element_type=F32) * dmat_ref[h]
            o = jnp.dot(a.astype(BF16), v, preferred_element_type=F32)
            o = o + jnp.dot(q, sf_ref[h, i], preferred_element_type=F32) * rdec_ref[h, QF]
            o = o + jnp.dot(q, sb_ref[h, i], preferred_element_type=F32) * rdec_ref[h, QB]
            mu = jnp.mean(o, axis=-1, keepdims=True)
            dev = o - mu
            var = jnp.mean(dev * dev, axis=-1, keepdims=True)
            y = dev * lax.rsqrt(var + EPS) * gnw_ref[:, hs]
            o_ref[sl, hs] = (y * gate_ref[sl, hs].astype(F32)).astype(BF16)
        return carry

    lax.fori_loop(0, n_chunks, out_step, 0, unroll=True)


def _retention(qr, kr, vr, gr, dec_f, dec_b, gnw, batch, seq_len, heads, chunk):
    n = qr.shape[0]
    d = RET_HEAD_DIM
    width = heads * d
    n_chunks = seq_len // chunk
    seq_spec = pl.BlockSpec((seq_len, width), lambda g, b: (b, g))
    dec_spec = pl.BlockSpec((heads, 8, LANES), lambda g, b: (g, 0, 0))
    state_scratch = pltpu.VMEM((heads, n_chunks, d, d), BF16)
    return pl.pallas_call(
        functools.partial(_retention_body, heads=heads, c=chunk),
        grid=(RET_HEADS // heads, batch),
        in_specs=[seq_spec, seq_spec, seq_spec, seq_spec, dec_spec, dec_spec,
                  pl.BlockSpec((1, width), lambda g, b: (0, g))],
        out_specs=seq_spec,
        out_shape=jax.ShapeDtypeStruct((n, RET_WIDTH), BF16),
        scratch_shapes=[pltpu.VMEM((heads, n_chunks, 2 * d, d), F32),
                        state_scratch, state_scratch,
                        pltpu.VMEM((heads, chunk, chunk), F32),
                        pltpu.VMEM((heads, 4, chunk, d), F32)],
        compiler_params=pltpu.CompilerParams(
            dimension_semantics=("arbitrary", "arbitrary"),
            vmem_limit_bytes=VMEM_LIMIT_BYTES),
        name="retention",
    )(qr, kr, vr, gr, dec_f, dec_b, gnw)


def _merge_out_body(ua_ref, ub_ref, gm_ref, x_ref, wa_ref, wb_ref, wo_ref, fg_ref, o_ref, *, final):
    d = x_ref.shape[1]
    ya = jnp.dot(ua_ref[...], wa_ref[0].astype(BF16), preferred_element_type=F32)
    yb = jnp.dot(ub_ref[...], wb_ref[0].astype(BF16), preferred_element_type=F32)
    merged = gm_ref[:, :d].astype(F32) * ya + gm_ref[:, d:].astype(F32) * yb
    xn = x_ref[...] + jnp.dot(merged.astype(BF16), wo_ref[0].astype(BF16), preferred_element_type=F32)
    if final:
        xn = xn * lax.rsqrt(jnp.mean(xn * xn, axis=-1, keepdims=True) + EPS) * fg_ref[...]
    o_ref[...] = xn


def _merge_out(ua, ub, gm, x2, wa, wb, wo, layer, fg, tm, final):
    n, d = x2.shape
    row = lambda i: (i, 0)
    const = lambda i: (0, 0)
    of_layer = lambda i: (layer, 0, 0)
    return pl.pallas_call(
        functools.partial(_merge_out_body, final=final),
        grid=(n // tm,),
        in_specs=[
            pl.BlockSpec((tm, ATTN_WIDTH), row),
            pl.BlockSpec((tm, RET_WIDTH), row),
            pl.BlockSpec((tm, 2 * d), row),
            pl.BlockSpec((tm, d), row),
            pl.BlockSpec((1, ATTN_WIDTH, d), of_layer),
            pl.BlockSpec((1, RET_WIDTH, d), of_layer),
            pl.BlockSpec((1, d, d), of_layer),
            pl.BlockSpec((1, d), const),
        ],
        out_specs=pl.BlockSpec((tm, d), row),
        out_shape=jax.ShapeDtypeStruct((n, d), F32),
        compiler_params=pltpu.CompilerParams(
            dimension_semantics=("arbitrary",), vmem_limit_bytes=VMEM_LIMIT_BYTES),
        name="merge_out",
    )(ua, ub, gm, x2, wa, wb, wo, fg)


def _tile_choices(seq_len):
    tm = 512 if seq_len % 512 == 0 else 128
    tm_out = 1024 if seq_len % 1024 == 0 else tm
    tq = 128
    tk = 512 if seq_len % 512 == 0 else 128
    ret_chunk = 256 if seq_len % 256 == 0 else RET_CHUNK
    attn_blocks = 4 if seq_len % (4 * tq) == 0 else 2
    return tm, tm_out, tq, tk, ret_chunk, attn_blocks


def kernel(x, norm_g, w_in, attn_q_norm, attn_k_norm, ret_decay_fwd, ret_decay_bwd, ret_gn_w,
           w_branch_attn, w_branch_ret, w_out, final_norm_g):
    b, s, d = x.shape
    depth = norm_g.shape[0]
    assert s % RET_CHUNK == 0 and s % GRID_W == 0
    tm, tm_out, tq, tk, ret_chunk, attn_blocks = _tile_choices(s)
    assert s % (attn_blocks * tq) == 0

    tabs_a = _rope_tables(s, ATTN_HEAD_DIM)
    tabs_r = _rope_tables(s, RET_HEAD_DIM)
    head_id = jnp.arange(LANES) // ATTN_HEAD_DIM
    gsum = (head_id[:, None] == head_id[None, :]).astype(BF16)
    gsum = jnp.concatenate([gsum, gsum], axis=0)
    head_id2 = jnp.arange(2 * LANES) // ATTN_HEAD_DIM
    gsum2 = (head_id2[:, None] == head_id2[None, :]).astype(BF16)
    gsum2 = jnp.concatenate([gsum2, gsum2], axis=0)

    x2 = x.reshape(b * s, d)
    for layer in range(depth):
        qg = jnp.tile(attn_q_norm[layer], LANES // ATTN_HEAD_DIM)[None, :]
        kg = jnp.tile(attn_k_norm[layer], LANES // ATTN_HEAD_DIM)[None, :]
        qa, ka, va, ga, qr, kr, vr, gr, gm = _in_proj(
            x2, norm_g[layer][None, :], w_in, layer, qg, kg,
            tabs_a, tabs_r, gsum, gsum2, s, tm)
        ua = _attention(qa, ka, va, ga, b, s, tq, tk, attn_blocks)
        dec_f = jnp.broadcast_to(ret_decay_fwd[layer].astype(F32)[:, None, None], (RET_HEADS, 8, LANES))
        dec_b = jnp.broadcast_to(ret_decay_bwd[layer].astype(F32)[:, None, None], (RET_HEADS, 8, LANES))
        ub = _retention(qr, kr, vr, gr, dec_f, dec_b, ret_gn_w[layer][None, :], b, s,
                        RET_HEADS_PER_STEP, ret_chunk)
        x2 = _merge_out(ua, ub, gm, x2, w_branch_attn, w_branch_ret, w_out, layer,
                        final_norm_g[None, :], tm_out, final=(layer == depth - 1))
    return x2.reshape(b, s, d)
```

```python
import functools

import jax
import jax.numpy as jnp
import numpy as np
from jax import lax
from jax.experimental import pallas as pl
from jax.experimental.pallas import tpu as pltpu

F32 = jnp.float32
BF16 = jnp.bfloat16

GRID_W = 64
ATTN_Q_HEADS = 8
ATTN_KV_HEADS = 2
ATTN_HEAD_DIM = 64
ATTN_GROUP = ATTN_Q_HEADS // ATTN_KV_HEADS
ATTN_WIDTH = ATTN_Q_HEADS * ATTN_HEAD_DIM
ATTN_KV_WIDTH = ATTN_KV_HEADS * ATTN_HEAD_DIM
GROUP_WIDTH = ATTN_GROUP * ATTN_HEAD_DIM
RET_HEADS = 4
RET_HEAD_DIM = 128
RET_WIDTH = RET_HEADS * RET_HEAD_DIM
RET_CHUNK = 128
RET_HEADS_PER_STEP = 2
ROPE_THETA = 10000.0
EPS = 1e-6
LANES = 128

VMEM_LIMIT_BYTES = 56 * 1024 * 1024

OFF_QA = 0
OFF_KA = OFF_QA + ATTN_WIDTH
OFF_VA = OFF_KA + ATTN_KV_WIDTH
OFF_GA = OFF_VA + ATTN_KV_WIDTH
OFF_QR = OFF_GA + ATTN_WIDTH
OFF_KR = OFF_QR + RET_WIDTH
OFF_VR = OFF_KR + RET_WIDTH
OFF_GR = OFF_VR + RET_WIDTH
OFF_GM = OFF_GR + RET_WIDTH


def _sigmoid(x):
    return 1.0 / (1.0 + jnp.exp(-x))


def _rope_tables(seq_len, head_dim):
    n_rows = seq_len // GRID_W
    row = np.repeat(np.arange(n_rows, dtype=np.float64), GRID_W)
    col = np.tile(np.arange(GRID_W, dtype=np.float64), n_rows)
    d_axis = head_dim // 2
    inv_freq = ROPE_THETA ** (-np.arange(0, d_axis, 2, dtype=np.float64) / d_axis)
    ang = np.concatenate([row[:, None] * inv_freq, col[:, None] * inv_freq], axis=-1)
    quarter = head_dim // 4
    d = np.arange(LANES) % head_dim
    axis = d // d_axis
    half = (d % d_axis) // quarter
    freq = d % quarter
    ang_l = ang[:, axis * quarter + freq]
    sign = np.where(half == 0, -1.0, 1.0)
    return (jnp.asarray(np.cos(ang_l), dtype=F32), jnp.asarray(np.sin(ang_l) * sign, dtype=F32))


def _rope_chunk(xc, cos, sin_signed, quarter):
    lane = lax.broadcasted_iota(jnp.int32, xc.shape, 1)
    first_half = (lane % (2 * quarter)) < quarter
    partner = jnp.where(first_half,
                        pltpu.roll(xc, LANES - quarter, 1),
                        pltpu.roll(xc, quarter, 1))
    return xc * cos + partner * sin_signed


def _in_proj_body(x_ref, g_ref, w_ref, qg_ref, kg_ref, ca_ref, sa_ref, cr_ref, sr_ref, gsum_ref, gsum2_ref,
                  qa_ref, kt_ref, va_ref, ga_ref, qr_ref, kr_ref, vr_ref, gr_ref, gm_ref):
    x = x_ref[...]
    h = x * lax.rsqrt(jnp.mean(x * x, axis=-1, keepdims=True) + EPS) * g_ref[...]
    hb = h.astype(BF16)

    def proj(off, width):
        return jnp.dot(hb, w_ref[0, :, off:off + width].astype(BF16), preferred_element_type=F32)

    ca, sa = ca_ref[...], sa_ref[...]
    cr, sr = cr_ref[...], sr_ref[...]

    def head_sumsq(z, gsum):
        sq = z * z
        hi = sq.astype(BF16)
        lo = (sq - hi.astype(F32)).astype(BF16)
        return jnp.dot(jnp.concatenate([hi, lo], axis=1), gsum, preferred_element_type=F32)

    def norm_rope(zc, ss, gain):
        y = zc * lax.rsqrt(ss * (1.0 / ATTN_HEAD_DIM) + EPS) * gain
        return _rope_chunk(y, ca, sa, ATTN_HEAD_DIM // 4)

    zq = proj(OFF_QA, ATTN_WIDTH)
    qg = qg_ref[...]
    pair = 2 * LANES
    for c in range(ATTN_WIDTH // pair):
        zp = zq[:, c * pair:(c + 1) * pair]
        ss = head_sumsq(zp, gsum2_ref[...])
        for t in range(2):
            y = norm_rope(zp[:, t * LANES:(t + 1) * LANES], ss[:, t * LANES:(t + 1) * LANES], qg)
            lo_lane = c * pair + t * LANES
            qa_ref[:, lo_lane:lo_lane + LANES] = (y * (ATTN_HEAD_DIM ** -0.5 * 1.4426950408889634)).astype(BF16)

    zkv = proj(OFF_KA, 2 * ATTN_KV_WIDTH)
    zk = zkv[:, :LANES]
    kt = norm_rope(zk, head_sumsq(zk, gsum_ref[...]), kg_ref[...]).T.astype(BF16)
    for g in range(ATTN_KV_HEADS):
        head_t = kt[g * ATTN_HEAD_DIM:(g + 1) * ATTN_HEAD_DIM, :]
        for slot in range(ATTN_GROUP):
            r0 = g * GROUP_WIDTH + slot * ATTN_HEAD_DIM
            kt_ref[0, r0:r0 + ATTN_HEAD_DIM, :] = head_t
    vc = zkv[:, LANES:]
    low = lax.broadcasted_iota(jnp.int32, vc.shape, 1) < ATTN_HEAD_DIM
    va_ref[:, :LANES] = jnp.where(low, vc, 1.0).astype(BF16)
    va_ref[:, LANES:] = jnp.where(low, pltpu.roll(vc, ATTN_HEAD_DIM, 1), 1.0).astype(BF16)

    zg = proj(OFF_GA, ATTN_WIDTH)
    ga_ref[...] = (zg * _sigmoid(zg)).astype(BF16)

    zq = proj(OFF_QR, RET_WIDTH)
    for c in range(RET_HEADS):
        qr_ref[:, c * LANES:(c + 1) * LANES] = _rope_chunk(
            zq[:, c * LANES:(c + 1) * LANES], cr, sr, RET_HEAD_DIM // 4).astype(BF16)
    zk = proj(OFF_KR, RET_WIDTH)
    for c in range(RET_HEADS):
        y = _rope_chunk(zk[:, c * LANES:(c + 1) * LANES], cr, sr, RET_HEAD_DIM // 4)
        kr_ref[:, c * LANES:(c + 1) * LANES] = (y * (RET_HEAD_DIM ** -0.5)).astype(BF16)
    zg = proj(OFF_GR, RET_WIDTH)
    gr_ref[...] = (zg * _sigmoid(zg)).astype(BF16)

    d_model = x.shape[1]
    for c in range(2 * d_model // 512):
        zg = proj(OFF_GM + c * 512, 512)
        gm_ref[:, c * 512:(c + 1) * 512] = _sigmoid(zg).astype(BF16)
    vr_ref[...] = proj(OFF_VR, RET_WIDTH).astype(BF16)


def _in_proj(x2, g, w, layer, qg, kg, tabs_a, tabs_r, gsum, gsum2, seq_len, tm):
    n, d = x2.shape
    d_in = w.shape[2]
    blocks_per_seq = seq_len // tm
    row = lambda i: (i, 0)
    const = lambda i: (0, 0)
    tab = lambda i: (i % blocks_per_seq, 0)
    tab_spec = pl.BlockSpec((tm, LANES), tab)
    def rows_out(width):
        return pl.BlockSpec((tm, width), row), jax.ShapeDtypeStruct((n, width), BF16)

    kt_rows = ATTN_KV_HEADS * GROUP_WIDTH
    kt_out = (pl.BlockSpec((1, kt_rows, tm), lambda i: (i // blocks_per_seq, 0, i % blocks_per_seq)),
              jax.ShapeDtypeStruct((n // seq_len, kt_rows, seq_len), BF16))
    outs = [rows_out(ATTN_WIDTH), kt_out, rows_out(ATTN_KV_HEADS * LANES), rows_out(ATTN_WIDTH),
            rows_out(RET_WIDTH), rows_out(RET_WIDTH), rows_out(RET_WIDTH), rows_out(RET_WIDTH),
            rows_out(2 * d)]
    out_specs = [o[0] for o in outs]
    out_shape = [o[1] for o in outs]
    return pl.pallas_call(
        _in_proj_body,
        grid=(n // tm,),
        in_specs=[
            pl.BlockSpec((tm, d), row),
            pl.BlockSpec((1, d), const),
            pl.BlockSpec((1, d, d_in), lambda i: (layer, 0, 0), pipeline_mode=pl.Buffered(1)),
            pl.BlockSpec((1, LANES), const),
            pl.BlockSpec((1, LANES), const),
            tab_spec, tab_spec, tab_spec, tab_spec,
            pl.BlockSpec((2 * LANES, LANES), const),
            pl.BlockSpec((4 * LANES, 2 * LANES), const),
        ],
        out_specs=out_specs,
        out_shape=out_shape,
        compiler_params=pltpu.CompilerParams(
            dimension_semantics=("arbitrary",), vmem_limit_bytes=VMEM_LIMIT_BYTES),
        name="in_proj",
    )(x2, g, w, qg, kg, *tabs_a, *tabs_r, gsum, gsum2)


def _attention_body(qc_ref, qn_ref, kt_ref, ktn_ref, va_ref, gate_ref, o_ref,
                    s0_ref, s1_ref, m0_ref, m1_ref, *, tq, tk):
    seq_len = kt_ref.shape[2]
    n_chunks = seq_len // tk
    rows = ATTN_GROUP * tq
    first_step = (pl.program_id(0) == 0) & (pl.program_id(1) == 0) & (pl.program_id(2) == 0)

    def stack_heads(q):
        lane = lax.broadcasted_iota(jnp.int32, q.shape, 1)
        return jnp.concatenate(
            [jnp.where(lane // ATTN_HEAD_DIM == h, q, jnp.zeros_like(q)) for h in range(ATTN_GROUP)],
            axis=0)

    def score_chunk(qs, k_ref, s_ref, c, m_part):
        st = jnp.dot(qs, k_ref[0, :, c * tk:(c + 1) * tk], preferred_element_type=F32)
        s_ref[:, c * tk:(c + 1) * tk] = st
        for t in range(tk // LANES):
            m_part = jnp.maximum(m_part, st[:, t * LANES:(t + 1) * LANES])
        return m_part

    def row_max(m_part):
        return jnp.broadcast_to(jnp.max(m_part, axis=-1, keepdims=True), (rows, LANES))

    def pv_chunk(s_ref, m_rows, c, acc):
        m_tiled = jnp.concatenate([m_rows] * (tk // LANES), axis=1)
        p = jnp.exp2(s_ref[:, c * tk:(c + 1) * tk] - m_tiled).astype(BF16)
        return acc + jnp.dot(p, va_ref[c * tk:(c + 1) * tk, :], preferred_element_type=F32)

    def write_block(acc, r0):
        o = acc * (1.0 / pltpu.roll(acc, ATTN_HEAD_DIM, 1))
        low = lax.broadcasted_iota(jnp.int32, (tq, LANES), 1) < ATTN_HEAD_DIM
        out = jnp.concatenate(
            [jnp.where(low, o[h * tq:(h + 1) * tq], pltpu.roll(o[(h + 1) * tq:(h + 2) * tq], ATTN_HEAD_DIM, 1))
             for h in range(0, ATTN_GROUP, 2)], axis=1)
        o_ref[r0:r0 + tq, :] = (out * gate_ref[r0:r0 + tq, :].astype(F32)).astype(BF16)

    neg_inf = jnp.full((rows, LANES), -jnp.inf, F32)

    @pl.when(first_step)
    def _():
        qs = stack_heads(qc_ref[0:tq, :])
        m_part = neg_inf
        for c in range(n_chunks):
            m_part = score_chunk(qs, kt_ref, s0_ref, c, m_part)
        m0_ref[...] = m_part

    def half(q_next, k_next, s_cur, m_cur, s_next, m_next, r0):
        qs = stack_heads(q_next)
        m_rows = row_max(m_cur[...])
        m_part = neg_inf
        acc = jnp.zeros((rows, LANES), F32)
        for c in range(n_chunks + 1):
            if c < n_chunks:
                acc = pv_chunk(s_cur, m_rows, c, acc)
            if c == n_chunks:
                write_block(acc, r0)
            if c >= 1:
                m_part = score_chunk(qs, k_next, s_next, c - 1, m_part)
        m_next[...] = m_part

    n_blocks = qc_ref.shape[0] // tq
    for j in range(n_blocks):
        last = j + 1 == n_blocks
        q_next = qn_ref[...] if last else qc_ref[(j + 1) * tq:(j + 2) * tq, :]
        k_next = ktn_ref if last else kt_ref
        if j % 2 == 0:
            half(q_next, k_next, s0_ref, m0_ref, s1_ref, m1_ref, j * tq)
        else:
            half(q_next, k_next, s1_ref, m1_ref, s0_ref, m0_ref, j * tq)


def _attention(qa, ka_t, va_aug, ga, batch, seq_len, tq, tk, nb):
    n = qa.shape[0]
    qblocks = seq_len // tq
    groups = qblocks // nb
    pair_map = lambda b, g, i: (b * groups + i, g)
    n_sweeps = batch * ATTN_KV_HEADS

    def following(b, g, i):
        sweep = b * ATTN_KV_HEADS + g
        wraps = i == groups - 1
        at_end = wraps & (sweep == n_sweeps - 1)
        nxt = jnp.where(wraps & ~at_end, sweep + 1, sweep)
        blk = jnp.where(wraps, jnp.where(at_end, qblocks - 1, 0), nb * (i + 1))
        return nxt // ATTN_KV_HEADS, nxt % ATTN_KV_HEADS, blk

    def next_q_map(b, g, i):
        bn, gn, blk = following(b, g, i)
        return bn * qblocks + blk, gn

    def next_k_map(b, g, i):
        bn, gn, _ = following(b, g, i)
        return bn, gn, 0

    rows = ATTN_GROUP * tq
    return pl.pallas_call(
        functools.partial(_attention_body, tq=tq, tk=tk),
        grid=(batch, ATTN_KV_HEADS, groups),
        in_specs=[
            pl.BlockSpec((nb * tq, GROUP_WIDTH), pair_map),
            pl.BlockSpec((tq, GROUP_WIDTH), next_q_map),
            pl.BlockSpec((1, GROUP_WIDTH, seq_len), lambda b, g, i: (b, g, 0)),
            pl.BlockSpec((1, GROUP_WIDTH, seq_len), next_k_map),
            pl.BlockSpec((seq_len, LANES), lambda b, g, i: (b, g)),
            pl.BlockSpec((nb * tq, GROUP_WIDTH), pair_map),
        ],
        out_specs=pl.BlockSpec((nb * tq, GROUP_WIDTH), pair_map),
        out_shape=jax.ShapeDtypeStruct((n, ATTN_WIDTH), BF16),
        scratch_shapes=[pltpu.VMEM((rows, seq_len), F32), pltpu.VMEM((rows, seq_len), F32),
                        pltpu.VMEM((rows, LANES), F32), pltpu.VMEM((rows, LANES), F32)],
        compiler_params=pltpu.CompilerParams(
            dimension_semantics=("arbitrary", "arbitrary", "arbitrary"),
            vmem_limit_bytes=VMEM_LIMIT_BYTES),
        name="attention",
    )(qa, qa, ka_t, ka_t, va_aug, ga)


def _log_sigmoid(x):
    return jnp.minimum(x, 0.0) - jnp.log1p(jnp.exp(-jnp.abs(x)))


def _retention_body(q_ref, k_ref, v_ref, gate_ref, df_ref, db_ref, gnw_ref, o_ref,
                    u_ref, sf_ref, sb_ref, dmat_ref, rdec_ref, *, heads, c):
    d = RET_HEAD_DIM
    n_chunks = q_ref.shape[0] // c
    QF, QB, KF, KB = range(4)
    log_decay = [(_log_sigmoid(df_ref[h, 0:1, 0:1]), _log_sigmoid(db_ref[h, 0:1, 0:1]))
                 for h in range(heads)]
    chunk_decay = [(jnp.exp(c * lg_f), jnp.exp(c * lg_b)) for lg_f, lg_b in log_decay]

    @pl.when(pl.program_id(1) == 0)
    def _():
        row = lax.broadcasted_iota(jnp.int32, (c, c), 0).astype(F32)
        col = lax.broadcasted_iota(jnp.int32, (c, c), 1).astype(F32)
        diff = row - col
        rowd = lax.broadcasted_iota(jnp.int32, (c, d), 0).astype(F32)
        for h, (lg_f, lg_b) in enumerate(log_decay):
            dmat_ref[h] = jnp.where(diff >= 0, jnp.exp(jnp.maximum(diff, 0.0) * lg_f),
                                    jnp.exp(jnp.maximum(-diff, 0.0) * lg_b))
            rdec_ref[h, QF] = jnp.exp((rowd + 1.0) * lg_f)
            rdec_ref[h, QB] = jnp.exp((c - rowd) * lg_b)
            rdec_ref[h, KF] = jnp.exp((c - 1.0 - rowd) * lg_f)
            rdec_ref[h, KB] = jnp.exp(rowd * lg_b)

    def outer_step(j, carry):
        sl = pl.ds(pl.multiple_of(j * c, c), c)
        for h in range(heads):
            hs = slice(h * d, (h + 1) * d)
            k = k_ref[sl, hs].astype(F32)
            kk = jnp.concatenate([(k * rdec_ref[h, KF]).astype(BF16),
                                  (k * rdec_ref[h, KB]).astype(BF16)], axis=1)
            u_ref[h, j] = lax.dot_general(kk, v_ref[sl, hs], (((0,), (0,)), ((), ())),
                                          preferred_element_type=F32)
        return carry

    lax.fori_loop(0, n_chunks, outer_step, 0, unroll=True)

    def scan_step(j, states):
        i = n_chunks - 1 - j
        new_states = []
        for h in range(heads):
            s_f, s_b = states[2 * h], states[2 * h + 1]
            sf_ref[h, j] = s_f.astype(BF16)
            sb_ref[h, i] = s_b.astype(BF16)
            new_states.append(s_f * chunk_decay[h][0] + u_ref[h, j, 0:d, :])
            new_states.append(s_b * chunk_decay[h][1] + u_ref[h, i, d:2 * d, :])
        return tuple(new_states)

    lax.fori_loop(0, n_chunks, scan_step, (jnp.zeros((d, d), F32),) * (2 * heads), unroll=True)

    def out_step(i, carry):
        sl = pl.ds(pl.multiple_of(i * c, c), c)
        for h in range(heads):
            hs = slice(h * d, (h + 1) * d)
            q = q_ref[sl, hs]
            v = v_ref[sl, hs]
            a = lax.dot_general(q, k_ref[sl, hs], (((1,), (1,)), ((), ())),
                                preferred_element_type=F32) * dmat_ref[h]
            o = jnp.dot(a.astype(BF16), v, preferred_element_type=F32)
            o = o + jnp.dot(q, sf_ref[h, i], preferred_element_type=F32) * rdec_ref[h, QF]
            o = o + jnp.dot(q, sb_ref[h, i], preferred_element_type=F32) * rdec_ref[h, QB]
            mu = jnp.mean(o, axis=-1, keepdims=True)
            dev = o - mu
            var = jnp.mean(dev * dev, axis=-1, keepdims=True)
            y = dev * lax.rsqrt(var + EPS) * gnw_ref[:, hs]
            o_ref[sl, hs] = (y * gate_ref[sl, hs].astype(F32)).astype(BF16)
        return carry

    lax.fori_loop(0, n_chunks, out_step, 0, unroll=True)


def _retention(qr, kr, vr, gr, dec_f, dec_b, gnw, batch, seq_len, heads, chunk):
    n = qr.shape[0]
    d = RET_HEAD_DIM
    width = heads * d
    n_chunks = seq_len // chunk
    seq_spec = pl.BlockSpec((seq_len, width), lambda g, b: (b, g))
    dec_spec = pl.BlockSpec((heads, 8, LANES), lambda g, b: (g, 0, 0))
    state_scratch = pltpu.VMEM((heads, n_chunks, d, d), BF16)
    return pl.pallas_call(
        functools.partial(_retention_body, heads=heads, c=chunk),
        grid=(RET_HEADS // heads, batch),
        in_specs=[seq_spec, seq_spec, seq_spec, seq_spec, dec_spec, dec_spec,
                  pl.BlockSpec((1, width), lambda g, b: (0, g))],
        out_specs=seq_spec,
        out_shape=jax.ShapeDtypeStruct((n, RET_WIDTH), BF16),
        scratch_shapes=[pltpu.VMEM((heads, n_chunks, 2 * d, d), F32),
                        state_scratch, state_scratch,
                        pltpu.VMEM((heads, chunk, chunk), F32),
                        pltpu.VMEM((heads, 4, chunk, d), F32)],
        compiler_params=pltpu.CompilerParams(
            dimension_semantics=("arbitrary", "arbitrary"),
            vmem_limit_bytes=VMEM_LIMIT_BYTES),
        name="retention",
    )(qr, kr, vr, gr, dec_f, dec_b, gnw)


def _merge_out_body(ua_ref, ub_ref, gm_ref, x_ref, wa_ref, wb_ref, wo_ref, fg_ref, o_ref, *, final):
    d = x_ref.shape[1]
    ya = jnp.dot(ua_ref[...], wa_ref[0].astype(BF16), preferred_element_type=F32)
    yb = jnp.dot(ub_ref[...], wb_ref[0].astype(BF16), preferred_element_type=F32)
    merged = gm_ref[:, :d].astype(F32) * ya + gm_ref[:, d:].astype(F32) * yb
    xn = x_ref[...] + jnp.dot(merged.astype(BF16), wo_ref[0].astype(BF16), preferred_element_type=F32)
    if final:
        xn = xn * lax.rsqrt(jnp.mean(xn * xn, axis=-1, keepdims=True) + EPS) * fg_ref[...]
    o_ref[...] = xn


def _merge_out(ua, ub, gm, x2, wa, wb, wo, layer, fg, tm, final):
    n, d = x2.shape
    row = lambda i: (i, 0)
    const = lambda i: (0, 0)
    of_layer = lambda i: (layer, 0, 0)
    return pl.pallas_call(
        functools.partial(_merge_out_body, final=final),
        grid=(n // tm,),
        in_specs=[
            pl.BlockSpec((tm, ATTN_WIDTH), row),
            pl.BlockSpec((tm, RET_WIDTH), row),
            pl.BlockSpec((tm, 2 * d), row),
            pl.BlockSpec((tm, d), row),
            pl.BlockSpec((1, ATTN_WIDTH, d), of_layer),
            pl.BlockSpec((1, RET_WIDTH, d), of_layer),
            pl.BlockSpec((1, d, d), of_layer),
            pl.BlockSpec((1, d), const),
        ],
        out_specs=pl.BlockSpec((tm, d), row),
        out_shape=jax.ShapeDtypeStruct((n, d), F32),
        compiler_params=pltpu.CompilerParams(
            dimension_semantics=("arbitrary",), vmem_limit_bytes=VMEM_LIMIT_BYTES),
        name="merge_out",
    )(ua, ub, gm, x2, wa, wb, wo, fg)


def _tile_choices(seq_len):
    tm = 512 if seq_len % 512 == 0 else 128
    tm_out = 1024 if seq_len % 1024 == 0 else tm
    tq = 128
    tk = 512 if seq_len % 512 == 0 else 128
    ret_chunk = 256 if seq_len % 256 == 0 else RET_CHUNK
    attn_blocks = 4 if seq_len % (4 * tq) == 0 else 2
    return tm, tm_out, tq, tk, ret_chunk, attn_blocks


def kernel(x, norm_g, w_in, attn_q_norm, attn_k_norm, ret_decay_fwd, ret_decay_bwd, ret_gn_w,
           w_branch_attn, w_branch_ret, w_out, final_norm_g):
    b, s, d = x.shape
    depth = norm_g.shape[0]
    assert s % RET_CHUNK == 0 and s % GRID_W == 0
    tm, tm_out, tq, tk, ret_chunk, attn_blocks = _tile_choices(s)
    assert s % (attn_blocks * tq) == 0

    tabs_a = _rope_tables(s, ATTN_HEAD_DIM)
    tabs_r = _rope_tables(s, RET_HEAD_DIM)
    head_id = jnp.arange(LANES) // ATTN_HEAD_DIM
    gsum = (head_id[:, None] == head_id[None, :]).astype(BF16)
    gsum = jnp.concatenate([gsum, gsum], axis=0)
    head_id2 = jnp.arange(2 * LANES) // ATTN_HEAD_DIM
    gsum2 = (head_id2[:, None] == head_id2[None, :]).astype(BF16)
    gsum2 = jnp.concatenate([gsum2, gsum2], axis=0)

    x2 = x.reshape(b * s, d)
    for layer in range(depth):
        qg = jnp.tile(attn_q_norm[layer], LANES // ATTN_HEAD_DIM)[None, :]
        kg = jnp.tile(attn_k_norm[layer], LANES // ATTN_HEAD_DIM)[None, :]
        qa, ka, va, ga, qr, kr, vr, gr, gm = _in_proj(
            x2, norm_g[layer][None, :], w_in, layer, qg, kg,
            tabs_a, tabs_r, gsum, gsum2, s, tm)
        ua = _attention(qa, ka, va, ga, b, s, tq, tk, attn_blocks)
        dec_f = jnp.broadcast_to(ret_decay_fwd[layer].astype(F32)[:, None, None], (RET_HEADS, 8, LANES))
        dec_b = jnp.broadcast_to(ret_decay_bwd[layer].astype(F32)[:, None, None], (RET_HEADS, 8, LANES))
        ub = _retention(qr, kr, vr, gr, dec_f, dec_b, ret_gn_w[layer][None, :], b, s,
                        RET_HEADS_PER_STEP, ret_chunk)
        x2 = _merge_out(ua, ub, gm, x2, w_branch_attn, w_branch_ret, w_out, layer,
                        final_norm_g[None, :], tm_out, final=(layer == depth - 1))
    return x2.reshape(b, s, d)
```

```python
import functools

import jax
import jax.numpy as jnp
import numpy as np
from jax import lax
from jax.experimental import pallas as pl
from jax.experimental.pallas import tpu as pltpu

F32 = jnp.float32
BF16 = jnp.bfloat16

GRID_W = 64
ATTN_Q_HEADS = 8
ATTN_KV_HEADS = 2
ATTN_HEAD_DIM = 64
ATTN_GROUP = ATTN_Q_HEADS // ATTN_KV_HEADS
ATTN_WIDTH = ATTN_Q_HEADS * ATTN_HEAD_DIM
ATTN_KV_WIDTH = ATTN_KV_HEADS * ATTN_HEAD_DIM
GROUP_WIDTH = ATTN_GROUP * ATTN_HEAD_DIM
RET_HEADS = 4
RET_HEAD_DIM = 128
RET_WIDTH = RET_HEADS * RET_HEAD_DIM
RET_CHUNK = 128
RET_HEADS_PER_STEP = 2
ROPE_THETA = 10000.0
EPS = 1e-6
LANES = 128

VMEM_LIMIT_BYTES = 56 * 1024 * 1024

OFF_QA = 0
OFF_KA = OFF_QA + ATTN_WIDTH
OFF_VA = OFF_KA + ATTN_KV_WIDTH
OFF_GA = OFF_VA + ATTN_KV_WIDTH
OFF_QR = OFF_GA + ATTN_WIDTH
OFF_KR = OFF_QR + RET_WIDTH
OFF_VR = OFF_KR + RET_WIDTH
OFF_GR = OFF_VR + RET_WIDTH
OFF_GM = OFF_GR + RET_WIDTH


def _sigmoid(x):
    return 1.0 / (1.0 + jnp.exp(-x))


def _rope_tables(seq_len, head_dim):
    n_rows = seq_len // GRID_W
    row = np.repeat(np.arange(n_rows, dtype=np.float64), GRID_W)
    col = np.tile(np.arange(GRID_W, dtype=np.float64), n_rows)
    d_axis = head_dim // 2
    inv_freq = ROPE_THETA ** (-np.arange(0, d_axis, 2, dtype=np.float64) / d_axis)
    ang = np.concatenate([row[:, None] * inv_freq, col[:, None] * inv_freq], axis=-1)
    quarter = head_dim // 4
    d = np.arange(LANES) % head_dim
    axis = d // d_axis
    half = (d % d_axis) // quarter
    freq = d % quarter
    ang_l = ang[:, axis * quarter + freq]
    sign = np.where(half == 0, -1.0, 1.0)
    return (jnp.asarray(np.cos(ang_l), dtype=F32), jnp.asarray(np.sin(ang_l) * sign, dtype=F32))


def _rope_chunk(xc, cos, sin_signed, quarter):
    lane = lax.broadcasted_iota(jnp.int32, xc.shape, 1)
    first_half = (lane % (2 * quarter)) < quarter
    partner = jnp.where(first_half,
                        pltpu.roll(xc, LANES - quarter, 1),
                        pltpu.roll(xc, quarter, 1))
    return xc * cos + partner * sin_signed


def _in_proj_body(x_ref, g_ref, w_ref, qg_ref, kg_ref, ca_ref, sa_ref, cr_ref, sr_ref, gsum_ref, gsum2_ref,
                  qa_ref, kt_ref, va_ref, ga_ref, qr_ref, kr_ref, vr_ref, gr_ref, gm_ref):
    x = x_ref[...]
    h = x * lax.rsqrt(jnp.mean(x * x, axis=-1, keepdims=True) + EPS) * g_ref[...]
    hb = h.astype(BF16)

    def proj(off, width):
        return jnp.dot(hb, w_ref[0, :, off:off + width].astype(BF16), preferred_element_type=F32)

    ca, sa = ca_ref[...], sa_ref[...]
    cr, sr = cr_ref[...], sr_ref[...]

    def head_sumsq(z, gsum):
        sq = z * z
        hi = sq.astype(BF16)
        lo = (sq - hi.astype(F32)).astype(BF16)
        return jnp.dot(jnp.concatenate([hi, lo], axis=1), gsum, preferred_element_type=F32)

    def norm_rope(zc, ss, gain):
        y = zc * lax.rsqrt(ss * (1.0 / ATTN_HEAD_DIM) + EPS) * gain
        return _rope_chunk(y, ca, sa, ATTN_HEAD_DIM // 4)

    zq = proj(OFF_QA, ATTN_WIDTH)
    qg = qg_ref[...]
    pair = 2 * LANES
    for c in range(ATTN_WIDTH // pair):
        zp = zq[:, c * pair:(c + 1) * pair]
        ss = head_sumsq(zp, gsum2_ref[...])
        for t in range(2):
            y = norm_rope(zp[:, t * LANES:(t + 1) * LANES], ss[:, t * LANES:(t + 1) * LANES], qg)
            lo_lane = c * pair + t * LANES
            qa_ref[:, lo_lane:lo_lane + LANES] = (y * (ATTN_HEAD_DIM ** -0.5 * 1.4426950408889634)).astype(BF16)

    zkv = proj(OFF_KA, 2 * ATTN_KV_WIDTH)
    zk = zkv[:, :LANES]
    kt = norm_rope(zk, head_sumsq(zk, gsum_ref[...]), kg_ref[...]).T.astype(BF16)
    for g in range(ATTN_KV_HEADS):
        head_t = kt[g * ATTN_HEAD_DIM:(g + 1) * ATTN_HEAD_DIM, :]
        for slot in range(ATTN_GROUP):
            r0 = g * GROUP_WIDTH + slot * ATTN_HEAD_DIM
            kt_ref[0, r0:r0 + ATTN_HEAD_DIM, :] = head_t
    vc = zkv[:, LANES:]
    low = lax.broadcasted_iota(jnp.int32, vc.shape, 1) < ATTN_HEAD_DIM
    va_ref[:, :LANES] = jnp.where(low, vc, 1.0).astype(BF16)
    va_ref[:, LANES:] = jnp.where(low, pltpu.roll(vc, ATTN_HEAD_DIM, 1), 1.0).astype(BF16)

    zg = proj(OFF_GA, ATTN_WIDTH)
    ga_ref[...] = (zg * _sigmoid(zg)).astype(BF16)

    zq = proj(OFF_QR, RET_WIDTH)
    for c in range(RET_HEADS):
        qr_ref[:, c * LANES:(c + 1) * LANES] = _rope_chunk(
            zq[:, c * LANES:(c + 1) * LANES], cr, sr, RET_HEAD_DIM // 4).astype(BF16)
    zk = proj(OFF_KR, RET_WIDTH)
    for c in range(RET_HEADS):
        y = _rope_chunk(zk[:, c * LANES:(c + 1) * LANES], cr, sr, RET_HEAD_DIM // 4)
        kr_ref[:, c * LANES:(c + 1) * LANES] = (y * (RET_HEAD_DIM ** -0.5)).astype(BF16)
    zg = proj(OFF_GR, RET_WIDTH)
    gr_ref[...] = (zg * _sigmoid(zg)).astype(BF16)

    d_model = x.shape[1]
    for c in range(2 * d_model // 512):
        zg = proj(OFF_GM + c * 512, 512)
        gm_ref[:, c * 512:(c + 1) * 512] = _sigmoid(zg).astype(BF16)
    vr_ref[...] = proj(OFF_VR, RET_WIDTH).astype(BF16)


def _in_proj(x2, g, w, layer, qg, kg, tabs_a, tabs_r, gsum, gsum2, seq_len, tm):
    n, d = x2.shape
    d_in = w.shape[2]
    blocks_per_seq = seq_len // tm
    row = lambda i: (i, 0)
    const = lambda i: (0, 0)
    tab = lambda i: (i % blocks_per_seq, 0)
    tab_spec = pl.BlockSpec((tm, LANES), tab)
    def rows_out(width):
        return pl.BlockSpec((tm, width), row), jax.ShapeDtypeStruct((n, width), BF16)

    kt_rows = ATTN_KV_HEADS * GROUP_WIDTH
    kt_out = (pl.BlockSpec((1, kt_rows, tm), lambda i: (i // blocks_per_seq, 0, i % blocks_per_seq)),
              jax.ShapeDtypeStruct((n // seq_len, kt_rows, seq_len), BF16))
    outs = [rows_out(ATTN_WIDTH), kt_out, rows_out(ATTN_KV_HEADS * LANES), rows_out(ATTN_WIDTH),
            rows_out(RET_WIDTH), rows_out(RET_WIDTH), rows_out(RET_WIDTH), rows_out(RET_WIDTH),
            rows_out(2 * d)]
    out_specs = [o[0] for o in outs]
    out_shape = [o[1] for o in outs]
    return pl.pallas_call(
        _in_proj_body,
        grid=(n // tm,),
        in_specs=[
            pl.BlockSpec((tm, d), row),
            pl.BlockSpec((1, d), const),
            pl.BlockSpec((1, d, d_in), lambda i: (layer, 0, 0), pipeline_mode=pl.Buffered(1)),
            pl.BlockSpec((1, LANES), const),
            pl.BlockSpec((1, LANES), const),
            tab_spec, tab_spec, tab_spec, tab_spec,
            pl.BlockSpec((2 * LANES, LANES), const),
            pl.BlockSpec((4 * LANES, 2 * LANES), const),
        ],
        out_specs=out_specs,
        out_shape=out_shape,
        compiler_params=pltpu.CompilerParams(
            dimension_semantics=("arbitrary",), vmem_limit_bytes=VMEM_LIMIT_BYTES),
        name="in_proj",
    )(x2, g, w, qg, kg, *tabs_a, *tabs_r, gsum, gsum2)


def _attention_body(qc_ref, qn_ref, kt_ref, ktn_ref, va_ref, gate_ref, o_ref,
                    s0_ref, s1_ref, m0_ref, m1_ref, *, tq, tk):
    seq_len = kt_ref.shape[2]
    n_chunks = seq_len // tk
    rows = ATTN_GROUP * tq
    first_step = (pl.program_id(0) == 0) & (pl.program_id(1) == 0) & (pl.program_id(2) == 0)

    def stack_heads(q):
        lane = lax.broadcasted_iota(jnp.int32, q.shape, 1)
        return jnp.concatenate(
            [jnp.where(lane // ATTN_HEAD_DIM == h, q, jnp.zeros_like(q)) for h in range(ATTN_GROUP)],
            axis=0)

    def score_chunk(qs, k_ref, s_ref, c, m_part):
        st = jnp.dot(qs, k_ref[0, :, c * tk:(c + 1) * tk], preferred_element_type=F32)
        s_ref[:, c * tk:(c + 1) * tk] = st
        for t in range(tk // LANES):
            m_part = jnp.maximum(m_part, st[:, t * LANES:(t + 1) * LANES])
        return m_part

    def row_max(m_part):
        return jnp.broadcast_to(jnp.max(m_part, axis=-1, keepdims=True), (rows, LANES))

    def pv_chunk(s_ref, m_rows, c, acc):
        m_tiled = jnp.concatenate([m_rows] * (tk // LANES), axis=1)
        p = jnp.exp2(s_ref[:, c * tk:(c + 1) * tk] - m_tiled).astype(BF16)
        return acc + jnp.dot(p, va_ref[c * tk:(c + 1) * tk, :], preferred_element_type=F32)

    def write_block(acc, r0):
        o = acc * (1.0 / pltpu.roll(acc, ATTN_HEAD_DIM, 1))
        low = lax.broadcasted_iota(jnp.int32, (tq, LANES), 1) < ATTN_HEAD_DIM
        out = jnp.concatenate(
            [jnp.where(low, o[h * tq:(h + 1) * tq], pltpu.roll(o[(h + 1) * tq:(h + 2) * tq], ATTN_HEAD_DIM, 1))
             for h in range(0, ATTN_GROUP, 2)], axis=1)
        o_ref[r0:r0 + tq, :] = (out * gate_ref[r0:r0 + tq, :].astype(F32)).astype(BF16)

    neg_inf = jnp.full((rows, LANES), -jnp.inf, F32)

    @pl.when(first_step)
    def _():
        qs = stack_heads(qc_ref[0:tq, :])
        m_part = neg_inf
        for c in range(n_chunks):
            m_part = score_chunk(qs, kt_ref, s0_ref, c, m_part)
        m0_ref[...] = m_part

    def half(q_next, k_next, s_cur, m_cur, s_next, m_next, r0, trail):
        qs = stack_heads(q_next)
        m_rows = row_max(m_cur[...])
        m_part = neg_inf
        acc = jnp.zeros((rows, LANES), F32)
        if trail:
            for c in range(n_chunks + 1):
                if c < n_chunks:
                    acc = pv_chunk(s_cur, m_rows, c, acc)
                else:
                    write_block(acc, r0)
                if c >= 1:
                    m_part = score_chunk(qs, k_next, s_next, c - 1, m_part)
        else:
            for c in range(n_chunks):
                m_part = score_chunk(qs, k_next, s_next, c, m_part)
                acc = pv_chunk(s_cur, m_rows, c, acc)
            write_block(acc, r0)
        m_next[...] = m_part

    n_blocks = qc_ref.shape[0] // tq
    for j in range(n_blocks):
        last = j + 1 == n_blocks
        q_next = qn_ref[...] if last else qc_ref[(j + 1) * tq:(j + 2) * tq, :]
        k_next = ktn_ref if last else kt_ref
        if j % 2 == 0:
            half(q_next, k_next, s0_ref, m0_ref, s1_ref, m1_ref, j * tq, last)
        else:
            half(q_next, k_next, s1_ref, m1_ref, s0_ref, m0_ref, j * tq, last)


def _attention(qa, ka_t, va_aug, ga, batch, seq_len, tq, tk, nb):
    n = qa.shape[0]
    qblocks = seq_len // tq
    groups = qblocks // nb
    pair_map = lambda b, g, i: (b * groups + i, g)
    n_sweeps = batch * ATTN_KV_HEADS

    def following(b, g, i):
        sweep = b * ATTN_KV_HEADS + g
        wraps = i == groups - 1
        at_end = wraps & (sweep == n_sweeps - 1)
        nxt = jnp.where(wraps & ~at_end, sweep + 1, sweep)
        blk = jnp.where(wraps, jnp.where(at_end, qblocks - 1, 0), nb * (i + 1))
        return nxt // ATTN_KV_HEADS, nxt % ATTN_KV_HEADS, blk

    def next_q_map(b, g, i):
        bn, gn, blk = following(b, g, i)
        return bn * qblocks + blk, gn

    def next_k_map(b, g, i):
        bn, gn, _ = following(b, g, i)
        return bn, gn, 0

    rows = ATTN_GROUP * tq
    return pl.pallas_call(
        functools.partial(_attention_body, tq=tq, tk=tk),
        grid=(batch, ATTN_KV_HEADS, groups),
        in_specs=[
            pl.BlockSpec((nb * tq, GROUP_WIDTH), pair_map),
            pl.BlockSpec((tq, GROUP_WIDTH), next_q_map),
            pl.BlockSpec((1, GROUP_WIDTH, seq_len), lambda b, g, i: (b, g, 0)),
            pl.BlockSpec((1, GROUP_WIDTH, seq_len), next_k_map),
            pl.BlockSpec((seq_len, LANES), lambda b, g, i: (b, g)),
            pl.BlockSpec((nb * tq, GROUP_WIDTH), pair_map),
        ],
        out_specs=pl.BlockSpec((nb * tq, GROUP_WIDTH), pair_map),
        out_shape=jax.ShapeDtypeStruct((n, ATTN_WIDTH), BF16),
        scratch_shapes=[pltpu.VMEM((rows, seq_len), F32), pltpu.VMEM((rows, seq_len), F32),
                        pltpu.VMEM((rows, LANES), F32), pltpu.VMEM((rows, LANES), F32)],
        compiler_params=pltpu.CompilerParams(
            dimension_semantics=("arbitrary", "arbitrary", "arbitrary"),
            vmem_limit_bytes=VMEM_LIMIT_BYTES),
        name="attention",
    )(qa, qa, ka_t, ka_t, va_aug, ga)


def _log_sigmoid(x):
    return jnp.minimum(x, 0.0) - jnp.log1p(jnp.exp(-jnp.abs(x)))


def _retention_body(q_ref, k_ref, v_ref, gate_ref, df_ref, db_ref, gnw_ref, o_ref,
                    u_ref, sf_ref, sb_ref, dmat_ref, rdec_ref, *, heads, c):
    d = RET_HEAD_DIM
    n_chunks = q_ref.shape[0] // c
    QF, QB, KF, KB = range(4)
    log_decay = [(_log_sigmoid(df_ref[h, 0:1, 0:1]), _log_sigmoid(db_ref[h, 0:1, 0:1]))
                 for h in range(heads)]
    chunk_decay = [(jnp.exp(c * lg_f), jnp.exp(c * lg_b)) for lg_f, lg_b in log_decay]

    @pl.when(pl.program_id(1) == 0)
    def _():
        row = lax.broadcasted_iota(jnp.int32, (c, c), 0).astype(F32)
        col = lax.broadcasted_iota(jnp.int32, (c, c), 1).astype(F32)
        diff = row - col
        rowd = lax.broadcasted_iota(jnp.int32, (c, d), 0).astype(F32)
        for h, (lg_f, lg_b) in enumerate(log_decay):
            dmat_ref[h] = jnp.where(diff >= 0, jnp.exp(jnp.maximum(diff, 0.0) * lg_f),
                                    jnp.exp(jnp.maximum(-diff, 0.0) * lg_b))
            rdec_ref[h, QF] = jnp.exp((rowd + 1.0) * lg_f)
            rdec_ref[h, QB] = jnp.exp((c - rowd) * lg_b)
            rdec_ref[h, KF] = jnp.exp((c - 1.0 - rowd) * lg_f)
            rdec_ref[h, KB] = jnp.exp(rowd * lg_b)

    def outer_step(j, carry):
        sl = pl.ds(pl.multiple_of(j * c, c), c)
        for h in range(heads):
            hs = slice(h * d, (h + 1) * d)
            k = k_ref[sl, hs].astype(F32)
            kk = jnp.concatenate([(k * rdec_ref[h, KF]).astype(BF16),
                                  (k * rdec_ref[h, KB]).astype(BF16)], axis=1)
            u_ref[h, j] = lax.dot_general(kk, v_ref[sl, hs], (((0,), (0,)), ((), ())),
                                          preferred_element_type=F32)
        return carry

    lax.fori_loop(0, n_chunks, outer_step, 0, unroll=True)

    def scan_step(j, states):
        i = n_chunks - 1 - j
        new_states = []
        for h in range(heads):
            s_f, s_b = states[2 * h], states[2 * h + 1]
            sf_ref[h, j] = s_f.astype(BF16)
            sb_ref[h, i] = s_b.astype(BF16)
            new_states.append(s_f * chunk_decay[h][0] + u_ref[h, j, 0:d, :])
            new_states.append(s_b * chunk_decay[h][1] + u_ref[h, i, d:2 * d, :])
        return tuple(new_states)

    lax.fori_loop(0, n_chunks, scan_step, (jnp.zeros((d, d), F32),) * (2 * heads), unroll=True)

    def out_step(i, carry):
        sl = pl.ds(pl.multiple_of(i * c, c), c)
        for h in range(heads):
            hs = slice(h * d, (h + 1) * d)
            q = q_ref[sl, hs]
            v = v_ref[sl, hs]
            a = lax.dot_general(q, k_ref[sl, hs], (((1,), (1,)), ((), ())),
                                preferred_element_type=F32) * dmat_ref[h]
            o = jnp.dot(a.astype(BF16), v, preferred_element_type=F32)
            o = o + jnp.dot(q, sf_ref[h, i], preferred_element_type=F32) * rdec_ref[h, QF]
            o = o + jnp.dot(q, sb_ref[h, i], preferred_element_type=F32) * rdec_ref[h, QB]
            mu = jnp.mean(o, axis=-1, keepdims=True)
            dev = o - mu
            var = jnp.mean(dev * dev, axis=-1, keepdims=True)
            y = dev * lax.rsqrt(var + EPS) * gnw_ref[:, hs]
            o_ref[sl, hs] = (y * gate_ref[sl, hs].astype(F32)).astype(BF16)
        return carry

    lax.fori_loop(0, n_chunks, out_step, 0, unroll=True)


def _retention(qr, kr, vr, gr, dec_f, dec_b, gnw, batch, seq_len, heads, chunk):
    n = qr.shape[0]
    d = RET_HEAD_DIM
    width = heads * d
    n_chunks = seq_len // chunk
    seq_spec = pl.BlockSpec((seq_len, width), lambda g, b: (b, g))
    dec_spec = pl.BlockSpec((heads, 8, LANES), lambda g, b: (g, 0, 0))
    state_scratch = pltpu.VMEM((heads, n_chunks, d, d), BF16)
    return pl.pallas_call(
        functools.partial(_retention_body, heads=heads, c=chunk),
        grid=(RET_HEADS // heads, batch),
        in_specs=[seq_spec, seq_spec, seq_spec, seq_spec, dec_spec, dec_spec,
                  pl.BlockSpec((1, width), lambda g, b: (0, g))],
        out_specs=seq_spec,
        out_shape=jax.ShapeDtypeStruct((n, RET_WIDTH), BF16),
        scratch_shapes=[pltpu.VMEM((heads, n_chunks, 2 * d, d), F32),
                        state_scratch, state_scratch,
                        pltpu.VMEM((heads, chunk, chunk), F32),
                        pltpu.VMEM((heads, 4, chunk, d), F32)],
        compiler_params=pltpu.CompilerParams(
            dimension_semantics=("arbitrary", "arbitrary"),
            vmem_limit_bytes=VMEM_LIMIT_BYTES),
        name="retention",
    )(qr, kr, vr, gr, dec_f, dec_b, gnw)


def _merge_out_body(ua_ref, ub_ref, gm_ref, x_ref, wa_ref, wb_ref, wo_ref, fg_ref, o_ref, *, final):
    d = x_ref.shape[1]
    ya = jnp.dot(ua_ref[...], wa_ref[0].astype(BF16), preferred_element_type=F32)
    yb = jnp.dot(ub_ref[...], wb_ref[0].astype(BF16), preferred_element_type=F32)
    merged = gm_ref[:, :d].astype(F32) * ya + gm_ref[:, d:].astype(F32) * yb
    xn = x_ref[...] + jnp.dot(merged.astype(BF16), wo_ref[0].astype(BF16), preferred_element_type=F32)
    if final:
        xn = xn * lax.rsqrt(jnp.mean(xn * xn, axis=-1, keepdims=True) + EPS) * fg_ref[...]
    o_ref[...] = xn


def _merge_out(ua, ub, gm, x2, wa, wb, wo, layer, fg, tm, final):
    n, d = x2.shape
    row = lambda i: (i, 0)
    const = lambda i: (0, 0)
    of_layer = lambda i: (layer, 0, 0)
    return pl.pallas_call(
        functools.partial(_merge_out_body, final=final),
        grid=(n // tm,),
        in_specs=[
            pl.BlockSpec((tm, ATTN_WIDTH), row),
            pl.BlockSpec((tm, RET_WIDTH), row),
            pl.BlockSpec((tm, 2 * d), row),
            pl.BlockSpec((tm, d), row),
            pl.BlockSpec((1, ATTN_WIDTH, d), of_layer),
            pl.BlockSpec((1, RET_WIDTH, d), of_layer),
            pl.BlockSpec((1, d, d), of_layer),
            pl.BlockSpec((1, d), const),
        ],
        out_specs=pl.BlockSpec((tm, d), row),
        out_shape=jax.ShapeDtypeStruct((n, d), F32),
        compiler_params=pltpu.CompilerParams(
            dimension_semantics=("arbitrary",), vmem_limit_bytes=VMEM_LIMIT_BYTES),
        name="merge_out",
    )(ua, ub, gm, x2, wa, wb, wo, fg)


def _tile_choices(seq_len):
    tm = 512 if seq_len % 512 == 0 else 128
    tm_out = 1024 if seq_len % 1024 == 0 else tm
    tq = 128
    tk = 512 if seq_len % 512 == 0 else 128
    ret_chunk = 256 if seq_len % 256 == 0 else RET_CHUNK
    attn_blocks = 4 if seq_len % (4 * tq) == 0 else 2
    return tm, tm_out, tq, tk, ret_chunk, attn_blocks


def kernel(x, norm_g, w_in, attn_q_norm, attn_k_norm, ret_decay_fwd, ret_decay_bwd, ret_gn_w,
           w_branch_attn, w_branch_ret, w_out, final_norm_g):
    b, s, d = x.shape
    depth = norm_g.shape[0]
    assert s % RET_CHUNK == 0 and s % GRID_W == 0
    tm, tm_out, tq, tk, ret_chunk, attn_blocks = _tile_choices(s)
    assert s % (attn_blocks * tq) == 0

    tabs_a = _rope_tables(s, ATTN_HEAD_DIM)
    tabs_r = _rope_tables(s, RET_HEAD_DIM)
    head_id = jnp.arange(LANES) // ATTN_HEAD_DIM
    gsum = (head_id[:, None] == head_id[None, :]).astype(BF16)
    gsum = jnp.concatenate([gsum, gsum], axis=0)
    head_id2 = jnp.arange(2 * LANES) // ATTN_HEAD_DIM
    gsum2 = (head_id2[:, None] == head_id2[None, :]).astype(BF16)
    gsum2 = jnp.concatenate([gsum2, gsum2], axis=0)

    x2 = x.reshape(b * s, d)
    for layer in range(depth):
        qg = jnp.tile(attn_q_norm[layer], LANES // ATTN_HEAD_DIM)[None, :]
        kg = jnp.tile(attn_k_norm[layer], LANES // ATTN_HEAD_DIM)[None, :]
        qa, ka, va, ga, qr, kr, vr, gr, gm = _in_proj(
            x2, norm_g[layer][None, :], w_in, layer, qg, kg,
            tabs_a, tabs_r, gsum, gsum2, s, tm)
        ua = _attention(qa, ka, va, ga, b, s, tq, tk, attn_blocks)
        dec_f = jnp.broadcast_to(ret_decay_fwd[layer].astype(F32)[:, None, None], (RET_HEADS, 8, LANES))
        dec_b = jnp.broadcast_to(ret_decay_bwd[layer].astype(F32)[:, None, None], (RET_HEADS, 8, LANES))
        ub = _retention(qr, kr, vr, gr, dec_f, dec_b, ret_gn_w[layer][None, :], b, s,
                        RET_HEADS_PER_STEP, ret_chunk)
        x2 = _merge_out(ua, ub, gm, x2, w_branch_attn, w_branch_ret, w_out, layer,
                        final_norm_g[None, :], tm_out, final=(layer == depth - 1))
    return x2.reshape(b, s, d)
```

```python
import functools

import jax
import jax.numpy as jnp
import numpy as np
from jax import lax
from jax.experimental import pallas as pl
from jax.experimental.pallas import tpu as pltpu

F32 = jnp.float32
BF16 = jnp.bfloat16

GRID_W = 64
ATTN_Q_HEADS = 8
ATTN_KV_HEADS = 2
ATTN_HEAD_DIM = 64
ATTN_GROUP = ATTN_Q_HEADS // ATTN_KV_HEADS
ATTN_WIDTH = ATTN_Q_HEADS * ATTN_HEAD_DIM
ATTN_KV_WIDTH = ATTN_KV_HEADS * ATTN_HEAD_DIM
GROUP_WIDTH = ATTN_GROUP * ATTN_HEAD_DIM
RET_HEADS = 4
RET_HEAD_DIM = 128
RET_WIDTH = RET_HEADS * RET_HEAD_DIM
RET_CHUNK = 128
RET_HEADS_PER_STEP = 2
ROPE_THETA = 10000.0
EPS = 1e-6
LANES = 128

VMEM_LIMIT_BYTES = 56 * 1024 * 1024

OFF_QA = 0
OFF_KA = OFF_QA + ATTN_WIDTH
OFF_VA = OFF_KA + ATTN_KV_WIDTH
OFF_GA = OFF_VA + ATTN_KV_WIDTH
OFF_QR = OFF_GA + ATTN_WIDTH
OFF_KR = OFF_QR + RET_WIDTH
OFF_VR = OFF_KR + RET_WIDTH
OFF_GR = OFF_VR + RET_WIDTH
OFF_GM = OFF_GR + RET_WIDTH


def _sigmoid(x):
    return 1.0 / (1.0 + jnp.exp(-x))


def _rope_tables(seq_len, head_dim):
    n_rows = seq_len // GRID_W
    row = np.repeat(np.arange(n_rows, dtype=np.float64), GRID_W)
    col = np.tile(np.arange(GRID_W, dtype=np.float64), n_rows)
    d_axis = head_dim // 2
    inv_freq = ROPE_THETA ** (-np.arange(0, d_axis, 2, dtype=np.float64) / d_axis)
    ang = np.concatenate([row[:, None] * inv_freq, col[:, None] * inv_freq], axis=-1)
    quarter = head_dim // 4
    d = np.arange(LANES) % head_dim
    axis = d // d_axis
    half = (d % d_axis) // quarter
    freq = d % quarter
    ang_l = ang[:, axis * quarter + freq]
    sign = np.where(half == 0, -1.0, 1.0)
    return (jnp.asarray(np.cos(ang_l), dtype=F32), jnp.asarray(np.sin(ang_l) * sign, dtype=F32))


def _rope_chunk(xc, cos, sin_signed, quarter):
    lane = lax.broadcasted_iota(jnp.int32, xc.shape, 1)
    first_half = (lane % (2 * quarter)) < quarter
    partner = jnp.where(first_half,
                        pltpu.roll(xc, LANES - quarter, 1),
                        pltpu.roll(xc, quarter, 1))
    return xc * cos + partner * sin_signed


def _in_proj_body(x_ref, g_ref, w_ref, qg_ref, kg_ref, ca_ref, sa_ref, cr_ref, sr_ref, gsum_ref, gsum2_ref,
                  qa_ref, kt_ref, va_ref, ga_ref, qr_ref, kr_ref, vr_ref, gr_ref, gm_ref):
    x = x_ref[...]
    h = x * lax.rsqrt(jnp.mean(x * x, axis=-1, keepdims=True) + EPS) * g_ref[...]
    hb = h.astype(BF16)

    def proj(off, width):
        return jnp.dot(hb, w_ref[0, :, off:off + width].astype(BF16), preferred_element_type=F32)

    ca, sa = ca_ref[...], sa_ref[...]
    cr, sr = cr_ref[...], sr_ref[...]

    def head_sumsq(z, gsum):
        sq = z * z
        hi = sq.astype(BF16)
        lo = (sq - hi.astype(F32)).astype(BF16)
        return jnp.dot(jnp.concatenate([hi, lo], axis=1), gsum, preferred_element_type=F32)

    def norm_rope(zc, ss, gain):
        y = zc * lax.rsqrt(ss * (1.0 / ATTN_HEAD_DIM) + EPS) * gain
        return _rope_chunk(y, ca, sa, ATTN_HEAD_DIM // 4)

    zq = proj(OFF_QA, ATTN_WIDTH)
    qg = qg_ref[...]
    pair = 2 * LANES
    for c in range(ATTN_WIDTH // pair):
        zp = zq[:, c * pair:(c + 1) * pair]
        ss = head_sumsq(zp, gsum2_ref[...])
        for t in range(2):
            y = norm_rope(zp[:, t * LANES:(t + 1) * LANES], ss[:, t * LANES:(t + 1) * LANES], qg)
            lo_lane = c * pair + t * LANES
            qa_ref[:, lo_lane:lo_lane + LANES] = (y * (ATTN_HEAD_DIM ** -0.5 * 1.4426950408889634)).astype(BF16)

    zkv = proj(OFF_KA, 2 * ATTN_KV_WIDTH)
    zk = zkv[:, :LANES]
    kt = norm_rope(zk, head_sumsq(zk, gsum_ref[...]), kg_ref[...]).T.astype(BF16)
    for g in range(ATTN_KV_HEADS):
        head_t = kt[g * ATTN_HEAD_DIM:(g + 1) * ATTN_HEAD_DIM, :]
        for slot in range(ATTN_GROUP):
            r0 = g * GROUP_WIDTH + slot * ATTN_HEAD_DIM
            kt_ref[0, r0:r0 + ATTN_HEAD_DIM, :] = head_t
    vc = zkv[:, LANES:]
    low = lax.broadcasted_iota(jnp.int32, vc.shape, 1) < ATTN_HEAD_DIM
    va_ref[:, :LANES] = jnp.where(low, vc, 1.0).astype(BF16)
    va_ref[:, LANES:] = jnp.where(low, pltpu.roll(vc, ATTN_HEAD_DIM, 1), 1.0).astype(BF16)

    zg = proj(OFF_GA, ATTN_WIDTH)
    ga_ref[...] = (zg * _sigmoid(zg)).astype(BF16)

    zq = proj(OFF_QR, RET_WIDTH)
    for c in range(RET_HEADS):
        qr_ref[:, c * LANES:(c + 1) * LANES] = _rope_chunk(
            zq[:, c * LANES:(c + 1) * LANES], cr, sr, RET_HEAD_DIM // 4).astype(BF16)
    zk = proj(OFF_KR, RET_WIDTH)
    for c in range(RET_HEADS):
        y = _rope_chunk(zk[:, c * LANES:(c + 1) * LANES], cr, sr, RET_HEAD_DIM // 4)
        kr_ref[:, c * LANES:(c + 1) * LANES] = (y * (RET_HEAD_DIM ** -0.5)).astype(BF16)
    zg = proj(OFF_GR, RET_WIDTH)
    gr_ref[...] = (zg * _sigmoid(zg)).astype(BF16)

    d_model = x.shape[1]
    for c in range(2 * d_model // 512):
        zg = proj(OFF_GM + c * 512, 512)
        gm_ref[:, c * 512:(c + 1) * 512] = _sigmoid(zg).astype(BF16)
    vr_ref[...] = proj(OFF_VR, RET_WIDTH).astype(BF16)


def _in_proj(x2, g, w, layer, qg, kg, tabs_a, tabs_r, gsum, gsum2, seq_len, tm):
    n, d = x2.shape
    d_in = w.shape[2]
    blocks_per_seq = seq_len // tm
    row = lambda i: (i, 0)
    const = lambda i: (0, 0)
    tab = lambda i: (i % blocks_per_seq, 0)
    tab_spec = pl.BlockSpec((tm, LANES), tab)
    def rows_out(width):
        return pl.BlockSpec((tm, width), row), jax.ShapeDtypeStruct((n, width), BF16)

    kt_rows = ATTN_KV_HEADS * GROUP_WIDTH
    kt_out = (pl.BlockSpec((1, kt_rows, tm), lambda i: (i // blocks_per_seq, 0, i % blocks_per_seq)),
              jax.ShapeDtypeStruct((n // seq_len, kt_rows, seq_len), BF16))
    outs = [rows_out(ATTN_WIDTH), kt_out, rows_out(ATTN_KV_HEADS * LANES), rows_out(ATTN_WIDTH),
            rows_out(RET_WIDTH), rows_out(RET_WIDTH), rows_out(RET_WIDTH), rows_out(RET_WIDTH),
            rows_out(2 * d)]
    out_specs = [o[0] for o in outs]
    out_shape = [o[1] for o in outs]
    return pl.pallas_call(
        _in_proj_body,
        grid=(n // tm,),
        in_specs=[
            pl.BlockSpec((tm, d), row),
            pl.BlockSpec((1, d), const),
            pl.BlockSpec((1, d, d_in), lambda i: (layer, 0, 0), pipeline_mode=pl.Buffered(1)),
            pl.BlockSpec((1, LANES), const),
            pl.BlockSpec((1, LANES), const),
            tab_spec, tab_spec, tab_spec, tab_spec,
            pl.BlockSpec((2 * LANES, LANES), const),
            pl.BlockSpec((4 * LANES, 2 * LANES), const),
        ],
        out_specs=out_specs,
        out_shape=out_shape,
        compiler_params=pltpu.CompilerParams(
            dimension_semantics=("arbitrary",), vmem_limit_bytes=VMEM_LIMIT_BYTES),
        name="in_proj",
    )(x2, g, w, qg, kg, *tabs_a, *tabs_r, gsum, gsum2)


def _attention_body(qc_ref, qn_ref, kt_ref, ktn_ref, va_ref, gate_ref, o_ref,
                    s0_ref, s1_ref, m0_ref, m1_ref, *, tq, tk):
    seq_len = kt_ref.shape[2]
    n_chunks = seq_len // tk
    rows = ATTN_GROUP * tq
    first_step = (pl.program_id(0) == 0) & (pl.program_id(1) == 0) & (pl.program_id(2) == 0)

    def stack_heads(q):
        lane = lax.broadcasted_iota(jnp.int32, q.shape, 1)
        return jnp.concatenate(
            [jnp.where(lane // ATTN_HEAD_DIM == h, q, jnp.zeros_like(q)) for h in range(ATTN_GROUP)],
            axis=0)

    def score_chunk(qs, k_ref, s_ref, c, m_part):
        st = jnp.dot(qs, k_ref[0, :, c * tk:(c + 1) * tk], preferred_element_type=F32)
        s_ref[:, c * tk:(c + 1) * tk] = st
        for t in range(tk // LANES):
            m_part = jnp.maximum(m_part, st[:, t * LANES:(t + 1) * LANES])
        return m_part

    def row_max(m_part):
        return jnp.broadcast_to(jnp.max(m_part, axis=-1, keepdims=True), (rows, LANES))

    def pv_chunk(s_ref, m_rows, c, acc):
        m_tiled = jnp.concatenate([m_rows] * (tk // LANES), axis=1)
        p = jnp.exp2(s_ref[:, c * tk:(c + 1) * tk] - m_tiled).astype(BF16)
        return acc + jnp.dot(p, va_ref[c * tk:(c + 1) * tk, :], preferred_element_type=F32)

    def write_block(acc, r0):
        o = acc * (1.0 / pltpu.roll(acc, ATTN_HEAD_DIM, 1))
        low = lax.broadcasted_iota(jnp.int32, (tq, LANES), 1) < ATTN_HEAD_DIM
        out = jnp.concatenate(
            [jnp.where(low, o[h * tq:(h + 1) * tq], pltpu.roll(o[(h + 1) * tq:(h + 2) * tq], ATTN_HEAD_DIM, 1))
             for h in range(0, ATTN_GROUP, 2)], axis=1)
        o_ref[r0:r0 + tq, :] = (out * gate_ref[r0:r0 + tq, :].astype(F32)).astype(BF16)

    neg_inf = jnp.full((rows, LANES), -jnp.inf, F32)

    @pl.when(first_step)
    def _():
        qs = stack_heads(qc_ref[0:tq, :])
        m_part = neg_inf
        for c in range(n_chunks):
            m_part = score_chunk(qs, kt_ref, s0_ref, c, m_part)
        m0_ref[...] = m_part

    def half(q_next, k_next, s_cur, m_cur, s_next, m_next, r0, trail):
        qs = stack_heads(q_next)
        m_rows = row_max(m_cur[...])
        m_part = neg_inf
        acc = jnp.zeros((rows, LANES), F32)
        if trail:
            for c in range(n_chunks + 1):
                if c < n_chunks:
                    acc = pv_chunk(s_cur, m_rows, c, acc)
                else:
                    write_block(acc, r0)
                if c >= 1:
                    m_part = score_chunk(qs, k_next, s_next, c - 1, m_part)
        else:
            for c in range(n_chunks):
                m_part = score_chunk(qs, k_next, s_next, c, m_part)
                acc = pv_chunk(s_cur, m_rows, c, acc)
            write_block(acc, r0)
        m_next[...] = m_part

    n_blocks = qc_ref.shape[0] // tq
    for j in range(n_blocks):
        last = j + 1 == n_blocks
        q_next = qn_ref[...] if last else qc_ref[(j + 1) * tq:(j + 2) * tq, :]
        k_next = ktn_ref if last else kt_ref
        if j % 2 == 0:
            half(q_next, k_next, s0_ref, m0_ref, s1_ref, m1_ref, j * tq, last)
        else:
            half(q_next, k_next, s1_ref, m1_ref, s0_ref, m0_ref, j * tq, last)


def _attention(qa, ka_t, va_aug, ga, batch, seq_len, tq, tk, nb):
    n = qa.shape[0]
    qblocks = seq_len // tq
    groups = qblocks // nb
    pair_map = lambda b, g, i: (b * groups + i, g)
    n_sweeps = batch * ATTN_KV_HEADS

    def following(b, g, i):
        sweep = b * ATTN_KV_HEADS + g
        wraps = i == groups - 1
        at_end = wraps & (sweep == n_sweeps - 1)
        nxt = jnp.where(wraps & ~at_end, sweep + 1, sweep)
        blk = jnp.where(wraps, jnp.where(at_end, qblocks - 1, 0), nb * (i + 1))
        return nxt // ATTN_KV_HEADS, nxt % ATTN_KV_HEADS, blk

    def next_q_map(b, g, i):
        bn, gn, blk = following(b, g, i)
        return bn * qblocks + blk, gn

    def next_k_map(b, g, i):
        bn, gn, _ = following(b, g, i)
        return bn, gn, 0

    rows = ATTN_GROUP * tq
    return pl.pallas_call(
        functools.partial(_attention_body, tq=tq, tk=tk),
        grid=(batch, ATTN_KV_HEADS, groups),
        in_specs=[
            pl.BlockSpec((nb * tq, GROUP_WIDTH), pair_map),
            pl.BlockSpec((tq, GROUP_WIDTH), next_q_map),
            pl.BlockSpec((1, GROUP_WIDTH, seq_len), lambda b, g, i: (b, g, 0)),
            pl.BlockSpec((1, GROUP_WIDTH, seq_len), next_k_map),
            pl.BlockSpec((seq_len, LANES), lambda b, g, i: (b, g)),
            pl.BlockSpec((nb * tq, GROUP_WIDTH), pair_map),
        ],
        out_specs=pl.BlockSpec((nb * tq, GROUP_WIDTH), pair_map),
        out_shape=jax.ShapeDtypeStruct((n, ATTN_WIDTH), BF16),
        scratch_shapes=[pltpu.VMEM((rows, seq_len), F32), pltpu.VMEM((rows, seq_len), F32),
                        pltpu.VMEM((rows, LANES), F32), pltpu.VMEM((rows, LANES), F32)],
        compiler_params=pltpu.CompilerParams(
            dimension_semantics=("arbitrary", "arbitrary", "arbitrary"),
            vmem_limit_bytes=VMEM_LIMIT_BYTES),
        name="attention",
    )(qa, qa, ka_t, ka_t, va_aug, ga)


def _log_sigmoid(x):
    return jnp.minimum(x, 0.0) - jnp.log1p(jnp.exp(-jnp.abs(x)))


def _retention_body(q_ref, k_ref, v_ref, gate_ref, df_ref, db_ref, gnw_ref, o_ref,
                    u_ref, sf_ref, sb_ref, dmat_ref, rdec_ref, *, heads, c):
    d = RET_HEAD_DIM
    n_chunks = q_ref.shape[0] // c
    QF, QB, KF, KB = range(4)
    log_decay = [(_log_sigmoid(df_ref[h, 0:1, 0:1]), _log_sigmoid(db_ref[h, 0:1, 0:1]))
                 for h in range(heads)]
    chunk_decay = [(jnp.exp(c * lg_f), jnp.exp(c * lg_b)) for lg_f, lg_b in log_decay]

    @pl.when(pl.program_id(1) == 0)
    def _():
        row = lax.broadcasted_iota(jnp.int32, (c, c), 0).astype(F32)
        col = lax.broadcasted_iota(jnp.int32, (c, c), 1).astype(F32)
        diff = row - col
        rowd = lax.broadcasted_iota(jnp.int32, (c, d), 0).astype(F32)
        for h, (lg_f, lg_b) in enumerate(log_decay):
            dmat_ref[h] = jnp.where(diff >= 0, jnp.exp(jnp.maximum(diff, 0.0) * lg_f),
                                    jnp.exp(jnp.maximum(-diff, 0.0) * lg_b))
            rdec_ref[h, QF] = jnp.exp((rowd + 1.0) * lg_f)
            rdec_ref[h, QB] = jnp.exp((c - rowd) * lg_b)
            rdec_ref[h, KF] = jnp.exp((c - 1.0 - rowd) * lg_f)
            rdec_ref[h, KB] = jnp.exp(rowd * lg_b)

    def outer_step(j, carry):
        sl = pl.ds(pl.multiple_of(j * c, c), c)
        for h in range(heads):
            hs = slice(h * d, (h + 1) * d)
            k = k_ref[sl, hs].astype(F32)
            kk = jnp.concatenate([(k * rdec_ref[h, KF]).astype(BF16),
                                  (k * rdec_ref[h, KB]).astype(BF16)], axis=1)
            u_ref[h, j] = lax.dot_general(kk, v_ref[sl, hs], (((0,), (0,)), ((), ())),
                                          preferred_element_type=F32)
        return carry

    lax.fori_loop(0, n_chunks, outer_step, 0, unroll=True)

    def scan_step(j, states):
        i = n_chunks - 1 - j
        new_states = []
        for h in range(heads):
            s_f, s_b = states[2 * h], states[2 * h + 1]
            sf_ref[h, j] = s_f.astype(BF16)
            sb_ref[h, i] = s_b.astype(BF16)
            new_states.append(s_f * chunk_decay[h][0] + u_ref[h, j, 0:d, :])
            new_states.append(s_b * chunk_decay[h][1] + u_ref[h, i, d:2 * d, :])
        return tuple(new_states)

    lax.fori_loop(0, n_chunks, scan_step, (jnp.zeros((d, d), F32),) * (2 * heads), unroll=True)

    def out_step(i, carry):
        sl = pl.ds(pl.multiple_of(i * c, c), c)
        for h in range(heads):
            hs = slice(h * d, (h + 1) * d)
            q = q_ref[sl, hs]
            v = v_ref[sl, hs]
            a = lax.dot_general(q, k_ref[sl, hs], (((1,), (1,)), ((), ())),
                                preferred_element_type=F32) * dmat_ref[h]
            o = jnp.dot(a.astype(BF16), v, preferred_element_type=F32)
            o = o + jnp.dot(q, sf_ref[h, i], preferred_element_type=F32) * rdec_ref[h, QF]
            o = o + jnp.dot(q, sb_ref[h, i], preferred_element_type=F32) * rdec_ref[h, QB]
            mu = jnp.mean(o, axis=-1, keepdims=True)
            dev = o - mu
            var = jnp.mean(dev * dev, axis=-1, keepdims=True)
            y = dev * lax.rsqrt(var + EPS) * gnw_ref[:, hs]
            o_ref[sl, hs] = (y * gate_ref[sl, hs].astype(F32)).astype(BF16)
        return carry

    lax.fori_loop(0, n_chunks, out_step, 0, unroll=True)


def _retention(qr, kr, vr, gr, dec_f, dec_b, gnw, batch, seq_len, heads, chunk):
    n = qr.shape[0]
    d = RET_HEAD_DIM
    width = heads * d
    n_chunks = seq_len // chunk
    seq_spec = pl.BlockSpec((seq_len, width), lambda g, b: (b, g))
    dec_spec = pl.BlockSpec((heads, 8, LANES), lambda g, b: (g, 0, 0))
    state_scratch = pltpu.VMEM((heads, n_chunks, d, d), BF16)
    return pl.pallas_call(
        functools.partial(_retention_body, heads=heads, c=chunk),
        grid=(RET_HEADS // heads, batch),
        in_specs=[seq_spec, seq_spec, seq_spec, seq_spec, dec_spec, dec_spec,
                  pl.BlockSpec((1, width), lambda g, b: (0, g))],
        out_specs=seq_spec,
        out_shape=jax.ShapeDtypeStruct((n, RET_WIDTH), BF16),
        scratch_shapes=[pltpu.VMEM((heads, n_chunks, 2 * d, d), F32),
                        state_scratch, state_scratch,
                        pltpu.VMEM((heads, chunk, chunk), F32),
                        pltpu.VMEM((heads, 4, chunk, d), F32)],
        compiler_params=pltpu.CompilerParams(
            dimension_semantics=("arbitrary", "arbitrary"),
            vmem_limit_bytes=VMEM_LIMIT_BYTES),
        name="retention",
    )(qr, kr, vr, gr, dec_f, dec_b, gnw)


def _merge_out_body(ua_ref, ub_ref, gm_ref, x_ref, wa_ref, wb_ref, wo_ref, fg_ref, o_ref, *, final):
    d = x_ref.shape[1]
    ya = jnp.dot(ua_ref[...], wa_ref[0].astype(BF16), preferred_element_type=F32)
    yb = jnp.dot(ub_ref[...], wb_ref[0].astype(BF16), preferred_element_type=F32)
    merged = gm_ref[:, :d].astype(F32) * ya + gm_ref[:, d:].astype(F32) * yb
    xn = x_ref[...] + jnp.dot(merged.astype(BF16), wo_ref[0].astype(BF16), preferred_element_type=F32)
    if final:
        xn = xn * lax.rsqrt(jnp.mean(xn * xn, axis=-1, keepdims=True) + EPS) * fg_ref[...]
    o_ref[...] = xn


def _merge_out(ua, ub, gm, x2, wa, wb, wo, layer, fg, tm, final):
    n, d = x2.shape
    row = lambda i: (i, 0)
    const = lambda i: (0, 0)
    of_layer = lambda i: (layer, 0, 0)
    return pl.pallas_call(
        functools.partial(_merge_out_body, final=final),
        grid=(n // tm,),
        in_specs=[
            pl.BlockSpec((tm, ATTN_WIDTH), row),
            pl.BlockSpec((tm, RET_WIDTH), row),
            pl.BlockSpec((tm, 2 * d), row),
            pl.BlockSpec((tm, d), row),
            pl.BlockSpec((1, ATTN_WIDTH, d), of_layer),
            pl.BlockSpec((1, RET_WIDTH, d), of_layer),
            pl.BlockSpec((1, d, d), of_layer),
            pl.BlockSpec((1, d), const),
        ],
        out_specs=pl.BlockSpec((tm, d), row),
        out_shape=jax.ShapeDtypeStruct((n, d), F32),
        compiler_params=pltpu.CompilerParams(
            dimension_semantics=("arbitrary",), vmem_limit_bytes=VMEM_LIMIT_BYTES),
        name="merge_out",
    )(ua, ub, gm, x2, wa, wb, wo, fg)


def _tile_choices(seq_len):
    tm = 512 if seq_len % 512 == 0 else 128
    tm_out = 1024 if seq_len % 1024 == 0 else tm
    tq = 256 if seq_len % 512 == 0 else 128
    attn_blocks = 2
    tk = 512 if seq_len % 512 == 0 else 128
    ret_chunk = 256 if seq_len % 256 == 0 else RET_CHUNK
    return tm, tm_out, tq, tk, ret_chunk, attn_blocks


def kernel(x, norm_g, w_in, attn_q_norm, attn_k_norm, ret_decay_fwd, ret_decay_bwd, ret_gn_w,
           w_branch_attn, w_branch_ret, w_out, final_norm_g):
    b, s, d = x.shape
    depth = norm_g.shape[0]
    assert s % RET_CHUNK == 0 and s % GRID_W == 0
    tm, tm_out, tq, tk, ret_chunk, attn_blocks = _tile_choices(s)
    assert s % (attn_blocks * tq) == 0

    tabs_a = _rope_tables(s, ATTN_HEAD_DIM)
    tabs_r = _rope_tables(s, RET_HEAD_DIM)
    head_id = jnp.arange(LANES) // ATTN_HEAD_DIM
    gsum = (head_id[:, None] == head_id[None, :]).astype(BF16)
    gsum = jnp.concatenate([gsum, gsum], axis=0)
    head_id2 = jnp.arange(2 * LANES) // ATTN_HEAD_DIM
    gsum2 = (head_id2[:, None] == head_id2[None, :]).astype(BF16)
    gsum2 = jnp.concatenate([gsum2, gsum2], axis=0)

    x2 = x.reshape(b * s, d)
    for layer in range(depth):
        qg = jnp.tile(attn_q_norm[layer], LANES // ATTN_HEAD_DIM)[None, :]
        kg = jnp.tile(attn_k_norm[layer], LANES // ATTN_HEAD_DIM)[None, :]
        qa, ka, va, ga, qr, kr, vr, gr, gm = _in_proj(
            x2, norm_g[layer][None, :], w_in, layer, qg, kg,
            tabs_a, tabs_r, gsum, gsum2, s, tm)
        ua = _attention(qa, ka, va, ga, b, s, tq, tk, attn_blocks)
        dec_f = jnp.broadcast_to(ret_decay_fwd[layer].astype(F32)[:, None, None], (RET_HEADS, 8, LANES))
        dec_b = jnp.broadcast_to(ret_decay_bwd[layer].astype(F32)[:, None, None], (RET_HEADS, 8, LANES))
        ub = _retention(qr, kr, vr, gr, dec_f, dec_b, ret_gn_w[layer][None, :], b, s,
                        RET_HEADS_PER_STEP, ret_chunk)
        x2 = _merge_out(ua, ub, gm, x2, w_branch_attn, w_branch_ret, w_out, layer,
                        final_norm_g[None, :], tm_out, final=(layer == depth - 1))
    return x2.reshape(b, s, d)
```
